```python
import math, functools
import jax, jax.numpy as jnp
from jax import lax
import numpy as np

D_MODEL = 1024
BATCH = 32
SEQ = 2048
DEPTH = 1
DEC_BATCH = 128
DEC_SEQ = 1
PAST_LEN = 16384
PAGE_SIZE = 128

MLA_HEADS = 8
MLA_Q_RANK = 256
MLA_KV_RANK = 128
MLA_NOPE = 64
MLA_ROPE = 32
MLA_V = 64
DIFF_HEADS = 4
DIFF_KV_HEADS = 2
DIFF_Q_PER_KV = DIFF_HEADS // DIFF_KV_HEADS
DIFF_HD = 64
DIFF_VD = 2 * DIFF_HD
N_GROUPS = 4
EXPERTS_PER_GROUP = 8
EXPERT_FF = 128
TOP_K = 2

ROPE_THETA = 10000.0
EPS = 1e-6
Q_BLOCK = 128
NEG_INIT = -1e30
MLA_SCALE = (MLA_NOPE + MLA_ROPE) ** -0.5
DIFF_SCALE = DIFF_HD ** -0.5
MIX_A = MLA_HEADS * MLA_V
MIX_B = DIFF_HEADS * DIFF_VD
MIX_WIDTH = MIX_A + MIX_B
COL_MLA_Q = MLA_Q_RANK
COL_MLA_KV = MLA_KV_RANK
COL_MLA_PE = MLA_ROPE
COL_DIFF_Q = DIFF_HEADS * 2 * DIFF_HD
COL_DIFF_K = DIFF_KV_HEADS * 2 * DIFF_HD
COL_DIFF_V = DIFF_KV_HEADS * DIFF_VD
IN_WIDTH = COL_MLA_Q + COL_MLA_KV + COL_MLA_PE + COL_DIFF_Q + COL_DIFF_K + COL_DIFF_V
IN_SPLITS = (COL_MLA_Q,
             COL_MLA_Q + COL_MLA_KV,
             COL_MLA_Q + COL_MLA_KV + COL_MLA_PE,
             COL_MLA_Q + COL_MLA_KV + COL_MLA_PE + COL_DIFF_Q,
             COL_MLA_Q + COL_MLA_KV + COL_MLA_PE + COL_DIFF_Q + COL_DIFF_K)

kernel_name = 'hymba_mla_diffattn_hmoe_step'


def _rms_norm(x, g):
    xf = x.astype(jnp.float32)
    y = xf * lax.rsqrt(jnp.mean(xf * xf, axis=-1, keepdims=True) + EPS)
    return (y * g.astype(jnp.float32)).astype(x.dtype)


def _rope(x, pos):
    d = x.shape[-1]
    half = d // 2
    inv = ROPE_THETA ** (-jnp.arange(half, dtype=jnp.float32) * 2.0 / d)
    ang = pos[:, None] * inv[None, :]
    shape = (1, pos.shape[0]) + (1,) * (x.ndim - 3) + (half,)
    cos = jnp.cos(ang).reshape(shape)
    sin = jnp.sin(ang).reshape(shape)
    xf = x.astype(jnp.float32)
    x1, x2 = xf[..., :half], xf[..., half:]
    return jnp.concatenate([x1 * cos - x2 * sin, x2 * cos + x1 * sin], axis=-1).astype(x.dtype)


def _mixer_inputs(h, pos, lp):
    B, T, _ = h.shape
    proj = h @ lp['w_in']
    cq, ckv_raw, kpe_raw, dq, dk, dv = jnp.split(proj, IN_SPLITS, axis=-1)
    cq = _rms_norm(cq, lp['g_mla_qa'])
    q = (cq @ lp['w_mla_uq']).reshape(B, T, MLA_HEADS, MLA_NOPE + MLA_ROPE)
    q_nope = _rms_norm(q[..., :MLA_NOPE], lp['g_mla_qn_nope'])
    q_pe = _rope(_rms_norm(q[..., MLA_NOPE:], lp['g_mla_qn_rope']), pos)
    ckv = _rms_norm(ckv_raw, lp['g_mla_kva'])
    kpe = _rope(_rms_norm(kpe_raw, lp['g_mla_kn_rope']), pos)
    dq = dq.reshape(B, T, DIFF_KV_HEADS, DIFF_Q_PER_KV, 2, DIFF_HD)
    dq = _rope(_rms_norm(dq, lp['g_diff_qn']), pos)
    dk = dk.reshape(B, T, DIFF_KV_HEADS, 2, DIFF_HD)
    dk = _rope(_rms_norm(dk, lp['g_diff_kn']), pos)
    dv = dv.reshape(B, T, DIFF_KV_HEADS, DIFF_VD)
    return q_nope, q_pe, ckv, kpe, dq, dk, dv


def _mla_keys(ckv, w_uk, g):
    return _rms_norm(jnp.einsum('bpc,chd->bphd', ckv, w_uk), g)


def _mla_scores(q_nope, q_pe, k_nope, kpe):
    s = jnp.einsum('bthd,bphd->bhtp', q_nope, k_nope) + jnp.einsum('bthr,bpr->bhtp', q_pe, kpe)
    return s.astype(jnp.float32) * MLA_SCALE


def _diff_scores(dq, dk):
    s = jnp.einsum('btgrmd,bpgmd->bgrmtp', dq, dk)
    return s.astype(jnp.float32) * DIFF_SCALE


def _mla_out(lat, w_uv):
    o = jnp.einsum('bhtc,chd->bthd', lat, w_uv)
    return o.reshape(o.shape[0], o.shape[1], MIX_A)


def _diff_merge(o, lam, g_subln, lam_init):
    d = o[:, :, :, 0] - lam * o[:, :, :, 1]
    d = jnp.transpose(d, (0, 3, 1, 2, 4))
    d = _rms_norm(d, g_subln) * (1.0 - lam_init)
    return d.reshape(d.shape[0], d.shape[1], MIX_B)


def _online_update(carry, s, v, eq):
    m, l, acc = carry
    m_new = jnp.maximum(m, jnp.max(s, axis=-1))
    p = jnp.exp(s - m_new[..., None])
    corr = jnp.exp(m - m_new)
    acc = acc * corr[..., None] + jnp.einsum(eq, p, v.astype(jnp.float32))
    return (m_new, l * corr + jnp.sum(p, axis=-1), acc)


def _attend_prompt(q_nope, q_pe, ckv, kpe, dq, dk, dv, lp, lam, lam_init):
    B, S = q_nope.shape[:2]
    nb = S // Q_BLOCK
    k_nope = _mla_keys(ckv, lp['w_mla_uk'], lp['g_mla_kn_nope'])
    key_pos = jnp.arange(S)
    ckv_f = ckv.astype(jnp.float32)
    dv_f = dv.astype(jnp.float32)

    def blocks(a):
        return a.reshape((B, nb, Q_BLOCK) + a.shape[2:]).swapaxes(0, 1)

    def one_block(args):
        qn, qp, q6, start = args
        qpos = start + jnp.arange(Q_BLOCK)
        mask = qpos[:, None] >= key_pos[None, :]
        s_a = jnp.where(mask, _mla_scores(qn, qp, k_nope, kpe), -jnp.inf)
        lat = jnp.einsum('bhtp,bpc->bhtc', jax.nn.softmax(s_a, axis=-1), ckv_f)
        o_a = _mla_out(lat, lp['w_mla_uv'])
        s_b = jnp.where(mask, _diff_scores(q6, dk), -jnp.inf)
        o_b = jnp.einsum('bgrmtp,bpge->bgrmte', jax.nn.softmax(s_b, axis=-1), dv_f)
        o_b = _diff_merge(o_b, lam, lp['g_diff_subln'], lam_init)
        return jnp.concatenate([o_a.astype(jnp.float32), o_b.astype(jnp.float32)], axis=-1)

    out = lax.map(one_block, (blocks(q_nope), blocks(q_pe), blocks(dq), jnp.arange(nb) * Q_BLOCK))
    return out.swapaxes(0, 1).reshape(B, S, MIX_WIDTH)


def _attend_sample(q_nope, q_pe, ckv, kpe, dq, dk, dv, lp, lam, lam_init, *,
                   cache_ckv, cache_kpe, cache_k, cache_v, page_table, layer):
    B, T = q_nope.shape[:2]
    f32 = jnp.float32
    causal = jnp.tril(jnp.ones((T, T), dtype=bool))
    w_uk, g_kn = lp['w_mla_uk'], lp['g_mla_kn_nope']
    eq_a = 'bhtp,bpc->bhtc'
    eq_b = 'bgrmtp,bpge->bgrmte'
    carry_a = (jnp.full((B, MLA_HEADS, T), NEG_INIT, f32), jnp.zeros((B, MLA_HEADS, T), f32),
               jnp.zeros((B, MLA_HEADS, T, MLA_KV_RANK), f32))
    shp_b = (B, DIFF_KV_HEADS, DIFF_Q_PER_KV, 2, T)
    carry_b = (jnp.full(shp_b, NEG_INIT, f32), jnp.zeros(shp_b, f32), jnp.zeros(shp_b + (DIFF_VD,), f32))

    def step(carry, idx):
        ca, cb = carry
        ckv_p = cache_ckv[layer, idx]
        kpe_p = cache_kpe[layer, idx]
        ca = _online_update(ca, _mla_scores(q_nope, q_pe, _mla_keys(ckv_p, w_uk, g_kn), kpe_p), ckv_p, eq_a)
        cb = _online_update(cb, _diff_scores(dq, cache_k[layer, idx]), cache_v[layer, idx], eq_b)
        return (ca, cb), None

    (carry_a, carry_b), _ = lax.scan(step, (carry_a, carry_b), page_table.T)
    s_a = jnp.where(causal, _mla_scores(q_nope, q_pe, _mla_keys(ckv, w_uk, g_kn), kpe), -jnp.inf)
    carry_a = _online_update(carry_a, s_a, ckv, eq_a)
    s_b = jnp.where(causal, _diff_scores(dq, dk), -jnp.inf)
    carry_b = _online_update(carry_b, s_b, dv, eq_b)
    o_a = _mla_out(carry_a[2] / carry_a[1][..., None], lp['w_mla_uv'])
    o_b = _diff_merge(carry_b[2] / carry_b[1][..., None], lam, lp['g_diff_subln'], lam_init)
    return jnp.concatenate([o_a.astype(f32), o_b.astype(f32)], axis=-1)


def _hier_moe(h, lp):
    B, T, D = h.shape
    hf = h.reshape(B * T, D)
    g_prob = jax.nn.softmax((hf @ lp['w_router_group'] + lp['b_router_group']).astype(jnp.float32), axis=-1)
    g_w, g_idx = lax.top_k(g_prob, 1)
    e_all = (jnp.einsum('nd,gde->nge', hf, lp['w_router_expert']) + lp['b_router_expert']).astype(jnp.float32)
    e_logits = jnp.take_along_axis(e_all, g_idx[:, :, None], axis=1)[:, 0]
    top_v, top_i = lax.top_k(jax.nn.softmax(e_logits, axis=-1), TOP_K)
    top_v = top_v / jnp.sum(top_v, axis=-1, keepdims=True)
    gate_in = jnp.sum(jax.nn.one_hot(top_i, EXPERTS_PER_GROUP, dtype=jnp.float32) * top_v[..., None], axis=1) * g_w
    out = jnp.zeros((B * T, D), jnp.float32)
    for g in range(N_GROUPS):
        gate = jnp.where(g_idx == g, gate_in, 0.0)
        a = jax.nn.silu(jnp.einsum('nd,edf->nef', hf, lp['w_exp_gate'][g])) * jnp.einsum('nd,edf->nef', hf, lp['w_exp_up'][g])
        out = out + jnp.einsum('nef,efd->nd', a * gate[:, :, None].astype(a.dtype), lp['w_exp_down'][g])
    return out.reshape(B, T, D).astype(h.dtype)


def _layer(x, c, pos, lp, lam, lam_init, attend):
    mod = (jax.nn.silu(c) @ lp['w_ada'] + lp['b_ada'])[:, None, :]
    sh1, sc1, gt1, sh2, sc2, gt2 = jnp.split(mod, 6, axis=-1)
    h = _rms_norm(x, lp['g_norm1']) * (1.0 + sc1) + sh1
    q_nope, q_pe, ckv, kpe, dq, dk, dv = _mixer_inputs(h, pos, lp)
    o = attend(q_nope, q_pe, ckv, kpe, dq, dk, dv, lp, lam, lam_init)
    x = x + gt1 * (o.astype(x.dtype) @ lp['w_o'])
    h2 = _rms_norm(x, lp['g_norm2']) * (1.0 + sc2) + sh2
    x = x + gt2 * _hier_moe(h2, lp)
    return x, ckv, kpe, dk, dv


def setup_inputs(seed: int = 0) -> dict:
    key = jax.random.key(seed)
    ks = iter(jax.random.split(key, 48))
    f32 = jnp.float32
    n_pages = PAST_LEN // PAGE_SIZE
    n_pool = (DEC_BATCH * n_pages * 5) // 4
    L = DEPTH
    D = D_MODEL

    def nrm(shape, scale=1.0):
        return jax.random.normal(next(ks), shape, f32) * scale

    def gain(n):
        return 1.0 + 0.05 * nrm((L, n))

    perm = jax.random.permutation(next(ks), n_pool)
    page_table = perm[:DEC_BATCH * n_pages].reshape(DEC_BATCH, n_pages).astype(jnp.int32)
    return {
        'x_prompt': nrm((BATCH, SEQ, D)),
        'x_sample': nrm((DEC_BATCH, DEC_SEQ, D)),
        'cache_mla_ckv': nrm((L, n_pool, PAGE_SIZE, MLA_KV_RANK)),
        'cache_mla_kpe': nrm((L, n_pool, PAGE_SIZE, MLA_ROPE)),
        'cache_diff_k': nrm((L, n_pool, PAGE_SIZE, DIFF_KV_HEADS, 2, DIFF_HD)),
        'cache_diff_v': nrm((L, n_pool, PAGE_SIZE, DIFF_KV_HEADS, DIFF_VD)),
        'page_table': page_table,
        'c_prompt': nrm((BATCH, D)),
        'c_sample': nrm((DEC_BATCH, D)),
        'w_ada': nrm((L, D, 6 * D), 0.5 * D ** -0.5),
        'b_ada': nrm((L, 6 * D), 0.01),
        'g_norm1': gain(D),
        'w_in': nrm((L, D, IN_WIDTH), D ** -0.5),
        'g_mla_qa': gain(MLA_Q_RANK),
        'w_mla_uq': nrm((L, MLA_Q_RANK, MLA_HEADS * (MLA_NOPE + MLA_ROPE)), MLA_Q_RANK ** -0.5),
        'g_mla_kva': gain(MLA_KV_RANK),
        'w_mla_uk': nrm((L, MLA_KV_RANK, MLA_HEADS, MLA_NOPE), MLA_KV_RANK ** -0.5),
        'w_mla_uv': nrm((L, MLA_KV_RANK, MLA_HEADS, MLA_V), MLA_KV_RANK ** -0.5),
        'g_mla_qn_nope': gain(MLA_NOPE),
        'g_mla_qn_rope': gain(MLA_ROPE),
        'g_mla_kn_nope': gain(MLA_NOPE),
        'g_mla_kn_rope': gain(MLA_ROPE),
        'g_diff_qn': gain(DIFF_HD),
        'g_diff_kn': gain(DIFF_HD),
        'lam_q1': nrm((L, DIFF_HD), 0.1),
        'lam_k1': nrm((L, DIFF_HD), 0.1),
        'lam_q2': nrm((L, DIFF_HD), 0.1),
        'lam_k2': nrm((L, DIFF_HD), 0.1),
        'g_diff_subln': gain(DIFF_VD),
        'w_o': nrm((L, MIX_WIDTH, D), MIX_WIDTH ** -0.5),
        'g_norm2': gain(D),
        'w_router_group': nrm((L, D, N_GROUPS), D ** -0.5),
        'b_router_group': nrm((L, N_GROUPS), 0.01),
        'w_router_expert': nrm((L, N_GROUPS, D, EXPERTS_PER_GROUP), D ** -0.5),
        'b_router_expert': nrm((L, N_GROUPS, EXPERTS_PER_GROUP), 0.01),
        'w_exp_gate': nrm((L, N_GROUPS, EXPERTS_PER_GROUP, D, EXPERT_FF), D ** -0.5),
        'w_exp_up': nrm((L, N_GROUPS, EXPERTS_PER_GROUP, D, EXPERT_FF), D ** -0.5),
        'w_exp_down': nrm((L, N_GROUPS, EXPERTS_PER_GROUP, EXPERT_FF, D), EXPERT_FF ** -0.5),
    }


def reference(x_prompt, x_sample, cache_mla_ckv, cache_mla_kpe, cache_diff_k, cache_diff_v, page_table,
              c_prompt, c_sample, w_ada, b_ada, g_norm1, w_in, g_mla_qa, w_mla_uq, g_mla_kva, w_mla_uk,
              w_mla_uv, g_mla_qn_nope, g_mla_qn_rope, g_mla_kn_nope, g_mla_kn_rope, g_diff_qn, g_diff_kn,
              lam_q1, lam_k1, lam_q2, lam_k2, g_diff_subln, w_o, g_norm2, w_router_group, b_router_group,
              w_router_expert, b_router_expert, w_exp_gate, w_exp_up, w_exp_down):
    past = page_table.shape[1] * PAGE_SIZE
    pos_p = jnp.arange(x_prompt.shape[1], dtype=jnp.float32)
    pos_s = jnp.arange(x_sample.shape[1], dtype=jnp.float32) + past
    xp, xs = x_prompt, x_sample
    p_ckv, p_kpe, p_k, p_v = [], [], [], []
    s_ckv, s_kpe, s_k, s_v = [], [], [], []
    for l in range(DEPTH):
        lp = {
            'w_ada': w_ada[l], 'b_ada': b_ada[l], 'g_norm1': g_norm1[l], 'w_in': w_in[l],
            'g_mla_qa': g_mla_qa[l], 'w_mla_uq': w_mla_uq[l], 'g_mla_kva': g_mla_kva[l],
            'w_mla_uk': w_mla_uk[l], 'w_mla_uv': w_mla_uv[l], 'g_mla_qn_nope': g_mla_qn_nope[l],
            'g_mla_qn_rope': g_mla_qn_rope[l], 'g_mla_kn_nope': g_mla_kn_nope[l],
            'g_mla_kn_rope': g_mla_kn_rope[l], 'g_diff_qn': g_diff_qn[l], 'g_diff_kn': g_diff_kn[l],
            'g_diff_subln': g_diff_subln[l], 'w_o': w_o[l], 'g_norm2': g_norm2[l],
            'w_router_group': w_router_group[l], 'b_router_group': b_router_group[l],
            'w_router_expert': w_router_expert[l], 'b_router_expert': b_router_expert[l],
            'w_exp_gate': w_exp_gate[l], 'w_exp_up': w_exp_up[l], 'w_exp_down': w_exp_down[l],
        }
        lam_init = 0.8 - 0.6 * math.exp(-0.3 * l)
        lam = (jnp.exp(jnp.sum(lam_q1[l] * lam_k1[l]).astype(jnp.float32))
               - jnp.exp(jnp.sum(lam_q2[l] * lam_k2[l]).astype(jnp.float32)) + lam_init)
        attend_s = functools.partial(_attend_sample, cache_ckv=cache_mla_ckv, cache_kpe=cache_mla_kpe,
                                     cache_k=cache_diff_k, cache_v=cache_diff_v, page_table=page_table, layer=l)
        xp, a1, a2, a3, a4 = _layer(xp, c_prompt, pos_p, lp, lam, lam_init, _attend_prompt)
        xs, b1, b2, b3, b4 = _layer(xs, c_sample, pos_s, lp, lam, lam_init, attend_s)
        p_ckv.append(a1); p_kpe.append(a2); p_k.append(a3); p_v.append(a4)
        s_ckv.append(b1); s_kpe.append(b2); s_k.append(b3); s_v.append(b4)
    return (xp, xs, jnp.stack(p_ckv), jnp.stack(p_kpe), jnp.stack(p_k), jnp.stack(p_v),
            jnp.stack(s_ckv), jnp.stack(s_kpe), jnp.stack(s_k), jnp.stack(s_v))
```

```python
import functools
import math

import jax
import jax.numpy as jnp
from jax import lax
from jax.experimental import pallas as pl
from jax.experimental.pallas import tpu as pltpu

F32 = jnp.float32
BF16 = jnp.bfloat16

MLA_HEADS = 8
MLA_Q_RANK = 256
MLA_KV_RANK = 128
MLA_NOPE = 64
MLA_ROPE = 32
MLA_V = 64
DIFF_HEADS = 4
DIFF_KV_HEADS = 2
DIFF_HD = 64
DIFF_VD = 128
N_GROUPS = 4
EXPERTS_PER_GROUP = 8
EXPERT_FF = 128
PAGE_SIZE = 128
ROPE_THETA = 10000.0
EPS = 1e-6
LOG2E = 1.4426950408889634
MLA_SCALE = (MLA_NOPE + MLA_ROPE) ** -0.5
DIFF_SCALE = DIFF_HD ** -0.5
NEG = -1e30

LANES = 128
MXU_DIM = 256
VMEM_LIMIT = 56 * 1024 * 1024
ROUTER_LANES = 128
N_ROUTED = N_GROUPS * EXPERTS_PER_GROUP
GROUP_FF = EXPERTS_PER_GROUP * EXPERT_FF
QK_W = MLA_HEADS * LANES
DQ_W = DIFF_HEADS * 2 * DIFF_HD
DK_W = DIFF_KV_HEADS * 2 * DIFF_HD
DV_W = DIFF_KV_HEADS * DIFF_VD
PROJ_W = MLA_Q_RANK + MLA_KV_RANK + LANES + DQ_W + DK_W + DV_W
DEC_PAGES = 8


def _dot(a, b):
    return jnp.dot(a, b, preferred_element_type=F32)


def _dot_nt(a, b):
    return lax.dot_general(a, b, (((1,), (1,)), ((), ())), preferred_element_type=F32)


def _split(a):
    hi = a.astype(BF16)
    lo = (a - hi.astype(F32)).astype(BF16)
    return hi, lo


def _rms(v, g):
    return v * lax.rsqrt(jnp.mean(v * v, axis=-1, keepdims=True) + EPS) * g


def _silu(v):
    return v / (1.0 + jnp.exp(-v))


def _cparams(sem):
    return pltpu.CompilerParams(dimension_semantics=sem, vmem_limit_bytes=VMEM_LIMIT)


def _const_spec(shape):
    nd = len(shape)
    return pl.BlockSpec(shape, lambda *_: (0,) * nd)


def _ada_kernel(c_ref, w_ref, b_ref, o_ref):
    s = _silu(c_ref[...])
    sh, sl = _split(s)
    wh, wl = _split(w_ref[...])
    o_ref[...] = _dot(sh, wh) + _dot(sh, wl) + _dot(sl, wh) + b_ref[...]


def _ada(c, w, b):
    m, d = c.shape
    n = w.shape[1]
    tn = 512
    return pl.pallas_call(
        _ada_kernel,
        grid=(n // tn,),
        in_specs=[_const_spec((m, d)), pl.BlockSpec((d, tn), lambda i: (0, i)), pl.BlockSpec((1, tn), lambda i: (0, i))],
        out_specs=pl.BlockSpec((m, tn), lambda i: (0, i)),
        out_shape=jax.ShapeDtypeStruct((m, n), F32),
        compiler_params=_cparams(("parallel",)),
        name="ada",
    )(c, w, b)


def _block_norm(v, bd, g):
    w = v.shape[1]
    sq = (v * v).astype(BF16)
    ms = jnp.concatenate([_dot(sq[:, i:i + MXU_DIM], bd) for i in range(0, w, MXU_DIM)], axis=1)
    return v * lax.rsqrt(ms + EPS) * g


def _rope(v, cos, sin, half, first):
    parts = []
    for i in range(0, v.shape[1], LANES):
        s = v[:, i:i + LANES]
        rot = jnp.where(first, pltpu.roll(s, LANES - half, 1), pltpu.roll(s, half, 1))
        parts.append(s * cos + rot * sin)
    return parts[0] if len(parts) == 1 else jnp.concatenate(parts, axis=1)


def _proj_kernel(x_ref, sc_ref, sh_ref, cosq_ref, sinq_ref, cosk_ref, sink_ref, cosd_ref, sind_ref,
                 g1_ref, win_ref, gqa_ref, wuq_ref, gq_ref, gkva_ref, wuk_ref, gk_ref, gkpe_ref, gdq_ref, gdk_ref,
                 bdq_ref, bdd_ref,
                 qf_ref, kf_ref, dq0_ref, dq1_ref, ckvb_ref, dkb_ref, dvb_ref, ckv_ref, kpe_ref, dk_ref, dv_ref):
    lane = lax.broadcasted_iota(jnp.int32, (1, LANES), 1)
    x = x_ref[...]
    h = _rms(x, g1_ref[...]) * (1.0 + sc_ref[...]) + sh_ref[...]
    proj = _dot(h.astype(BF16), win_ref[...])
    o_ckv = MLA_Q_RANK
    o_kpe = o_ckv + MLA_KV_RANK
    o_dq = o_kpe + LANES
    o_dk = o_dq + DQ_W
    o_dv = o_dk + DK_W

    cqn = _rms(proj[:, :MLA_Q_RANK], gqa_ref[...])
    q = _dot(cqn.astype(BF16), wuq_ref[...])
    qn = _block_norm(q, bdq_ref[...], gq_ref[...])
    q_first = (lane >= MLA_NOPE) & (lane < MLA_NOPE + MLA_ROPE // 2)
    qf = _rope(qn, cosq_ref[...], sinq_ref[...], MLA_ROPE // 2, q_first)
    qf_ref[...] = (qf * (MLA_SCALE * LOG2E)).astype(BF16)

    ckv = _rms(proj[:, o_ckv:o_kpe], gkva_ref[...])
    ckv_ref[...] = ckv
    ckvb = ckv.astype(BF16)
    ckvb_ref[...] = ckvb
    kr = proj[:, o_kpe:o_dq]
    kn = kr * lax.rsqrt(jnp.sum(kr * kr, axis=-1, keepdims=True) * (1.0 / MLA_ROPE) + EPS) * gkpe_ref[...]
    kpe = _rope(kn, cosk_ref[...], sink_ref[...], MLA_ROPE // 2, lane < MLA_ROPE // 2)
    kpe_ref[...] = kpe[:, :MLA_ROPE]

    kraw = _dot(ckvb, wuk_ref[...])
    knn = _block_norm(kraw, bdq_ref[...], gk_ref[...])
    kpe_at_rope = pltpu.roll(kpe, MLA_NOPE, 1)
    kf_ref[...] = jnp.concatenate(
        [knn[:, i:i + LANES] + kpe_at_rope for i in range(0, QK_W, LANES)], axis=1).astype(BF16)

    d_first = (lane & (DIFF_HD - 1)) < DIFF_HD // 2
    dq = _block_norm(proj[:, o_dq:o_dk], bdd_ref[...], gdq_ref[...])
    dq = _rope(dq, cosd_ref[...], sind_ref[...], DIFF_HD // 2, d_first) * (DIFF_SCALE * LOG2E)
    map0 = (lax.broadcasted_iota(jnp.int32, (1, DQ_W), 1) & (LANES - 1)) < DIFF_HD
    dq0_ref[...] = jnp.where(map0, dq, 0.0).astype(BF16)
    dq1_ref[...] = jnp.where(map0, 0.0, dq).astype(BF16)
    dk = _block_norm(proj[:, o_dk:o_dv], bdd_ref[...], gdk_ref[...])
    dk = _rope(dk, cosd_ref[...], sind_ref[...], DIFF_HD // 2, d_first)
    dk_ref[...] = dk
    dkb_ref[...] = dk.astype(BF16)
    dv = proj[:, o_dv:]
    dv_ref[...] = dv
    dvb_ref[...] = dv.astype(BF16)


def _proj(x3, sc, sh, tables, wts, tm):
    b, t, d = x3.shape
    nt = t // tm
    per_tok = sc.shape[1] != 1
    tab_rows = tables[0].shape[0]

    def tok_spec(w):
        return pl.BlockSpec((None, tm, w), lambda s, bb: (bb, s, 0))

    mod_spec = tok_spec(d) if per_tok else pl.BlockSpec((None, 1, d), lambda s, bb: (bb, 0, 0))
    tab_spec = (pl.BlockSpec((tm, LANES), lambda s, bb: (s, 0)) if tab_rows != 1
                else pl.BlockSpec((1, LANES), lambda s, bb: (0, 0)))
    in_specs = [tok_spec(d), mod_spec, mod_spec] + [tab_spec] * 6 + [_const_spec(w.shape) for w in wts]
    widths = [(QK_W, BF16), (QK_W, BF16), (DQ_W, BF16), (DQ_W, BF16), (MLA_KV_RANK, BF16), (DK_W, BF16),
              (DV_W, BF16), (MLA_KV_RANK, F32), (MLA_ROPE, F32), (DK_W, F32), (DV_W, F32)]
    return pl.pallas_call(
        _proj_kernel,
        grid=(nt, b),
        in_specs=in_specs,
        out_specs=[tok_spec(w) for w, _ in widths],
        out_shape=[jax.ShapeDtypeStruct((b, t, w), dt) for w, dt in widths],
        compiler_params=_cparams(("parallel", "parallel")),
        name="proj",
    )(x3, sc, sh, *tables, *wts)


def _lam(lq1_ref, lk1_ref, lq2_ref, lk2_ref, lam_init):
    a = jnp.sum(lq1_ref[...] * lk1_ref[...], axis=-1, keepdims=True)
    b = jnp.sum(lq2_ref[...] * lk2_ref[...], axis=-1, keepdims=True)
    return jnp.exp(a) - jnp.exp(b) + lam_init


def _attn_kernel(qf_ref, dq0_ref, dq1_ref, kf_ref, ckv_ref, dk_ref, dv_ref, wuv_ref, gsub_ref,
                 lq1_ref, lk1_ref, lq2_ref, lk2_ref, out_ref,
                 m_a, l_a, acc_a, m_d, l_d, acc_d, *, tq, lam_init):
    i = pl.program_id(1)
    j = pl.program_id(2)

    @pl.when(j == 0)
    def _():
        m_a[...] = jnp.full(m_a.shape, NEG, F32)
        l_a[...] = jnp.zeros(l_a.shape, F32)
        acc_a[...] = jnp.zeros(acc_a.shape, F32)
        m_d[...] = jnp.full(m_d.shape, NEG, F32)
        l_d[...] = jnp.zeros(l_d.shape, F32)
        acc_d[...] = jnp.zeros(acc_d.shape, F32)

    def online(s, m_ref, l_ref, acc_ref, idx, v):
        m_prev = m_ref[idx]
        m_new = jnp.maximum(m_prev, jnp.max(s, axis=-1, keepdims=True))
        alpha = jnp.exp2(m_prev - m_new)
        p = jnp.exp2(s - m_new)
        l_ref[idx] = alpha * l_ref[idx] + jnp.sum(p, axis=-1, keepdims=True)
        acc_ref[idx] = alpha * acc_ref[idx] + _dot(p.astype(BF16), v)
        m_ref[idx] = m_new

    def step(masked):
        if masked:
            keep4 = (lax.broadcasted_iota(jnp.int32, (4 * tq, tq), 1)
                     <= (lax.broadcasted_iota(jnp.int32, (4 * tq, tq), 0) & (tq - 1)))
            keep = lax.broadcasted_iota(jnp.int32, (tq, tq), 1) <= lax.broadcasted_iota(jnp.int32, (tq, tq), 0)
        ckv = ckv_ref[...]
        for h in range(MLA_HEADS):
            sl = slice(h * LANES, (h + 1) * LANES)
            s = _dot_nt(qf_ref[:, sl], kf_ref[:, sl])
            if masked:
                s = jnp.where(keep, s, NEG)
            online(s, m_a, l_a, acc_a, h, ckv)
        for g in range(DIFF_KV_HEADS):
            qs = []
            for r in range(DIFF_HEADS // DIFF_KV_HEADS):
                sl = slice((g * 2 + r) * LANES, (g * 2 + r + 1) * LANES)
                qs += [dq0_ref[:, sl], dq1_ref[:, sl]]
            q = jnp.concatenate(qs, axis=0)
            gs = slice(g * LANES, (g + 1) * LANES)
            s = _dot_nt(q, dk_ref[:, gs])
            if masked:
                s = jnp.where(keep4, s, NEG)
            online(s, m_d, l_d, acc_d, g, dv_ref[:, gs])

    @pl.when(j < i)
    def _():
        step(False)

    @pl.when(j == i)
    def _():
        step(True)
        lat = jnp.concatenate([(acc_a[h] / l_a[h]).astype(BF16) for h in range(MLA_HEADS)], axis=1)
        out_ref[:, :MLA_HEADS * MLA_V] = _dot(lat, wuv_ref[...]).astype(out_ref.dtype)
        lam = _lam(lq1_ref, lk1_ref, lq2_ref, lk2_ref, lam_init)
        for g in range(DIFF_KV_HEADS):
            o = acc_d[g] / l_d[g]
            for r in range(DIFF_HEADS // DIFF_KV_HEADS):
                d = o[(2 * r) * tq:(2 * r + 1) * tq] - lam * o[(2 * r + 1) * tq:(2 * r + 2) * tq]
                d = _rms(d, gsub_ref[...]) * (1.0 - lam_init)
                c0 = MLA_HEADS * MLA_V + (g * 2 + r) * DIFF_VD
                out_ref[:, c0:c0 + DIFF_VD] = d.astype(out_ref.dtype)


def _attn(qf, dq0, dq1, kf, ckvb, dkb, dvb, wuv_bd, gsub, lams, lam_init, tq):
    b, s, _ = qf.shape
    nq = s // tq
    assert tq & (tq - 1) == 0 and s % tq == 0

    def q_spec(w):
        return pl.BlockSpec((None, tq, w), lambda bb, i, j: (bb, i, 0))

    def k_spec(w):
        return pl.BlockSpec((None, tq, w), lambda bb, i, j: (bb, jnp.minimum(i, j), 0))

    mix_w = MLA_HEADS * MLA_V + DIFF_HEADS * DIFF_VD
    return pl.pallas_call(
        functools.partial(_attn_kernel, tq=tq, lam_init=lam_init),
        grid=(b, nq, nq),
        in_specs=[q_spec(QK_W), q_spec(DQ_W), q_spec(DQ_W), k_spec(QK_W), k_spec(MLA_KV_RANK), k_spec(DK_W),
                  k_spec(DV_W), _const_spec(wuv_bd.shape), _const_spec(gsub.shape)] + [_const_spec(l.shape) for l in lams],
        out_specs=q_spec(mix_w),
        out_shape=jax.ShapeDtypeStruct((b, s, mix_w), BF16),
        scratch_shapes=[pltpu.VMEM((MLA_HEADS, tq, 1), F32), pltpu.VMEM((MLA_HEADS, tq, 1), F32),
                        pltpu.VMEM((MLA_HEADS, tq, MLA_KV_RANK), F32),
                        pltpu.VMEM((DIFF_KV_HEADS, 4 * tq, 1), F32), pltpu.VMEM((DIFF_KV_HEADS, 4 * tq, 1), F32),
                        pltpu.VMEM((DIFF_KV_HEADS, 4 * tq, DIFF_VD), F32)],
        compiler_params=_cparams(("parallel", "parallel", "arbitrary")),
        name="attn",
    )(qf, dq0, dq1, kf, ckvb, dkb, dvb, wuv_bd, gsub, *lams)


def _qabs_kernel(qf_ref, gk_ref, wt_ref, qa_ref):
    for h in range(MLA_HEADS):
        sl = slice(h * LANES, (h + 1) * LANES)
        qg = (qf_ref[:, sl].astype(F32) * gk_ref[...]).astype(BF16)
        qa_ref[:, sl] = _dot(qg, wt_ref[h])


def _qabs(qf, gk_row, wuk_t):
    n = qf.shape[0]
    return pl.pallas_call(
        _qabs_kernel,
        grid=(1,),
        in_specs=[_const_spec(qf.shape), _const_spec(gk_row.shape), _const_spec(wuk_t.shape)],
        out_specs=_const_spec((n, QK_W)),
        out_shape=jax.ShapeDtypeStruct((n, QK_W), F32),
        compiler_params=_cparams(("arbitrary",)),
        name="qabs",
    )(qf, gk_row, wuk_t)


def _dec_kernel(pt_ref, qa_ref, qp_ref, qbd_ref, qf8_ref, kf8_ref, dkrow_ref, ckvrow_ref, dvrow_ref,
                wukt_ref, wuv_ref, gsub_ref, lq1_ref, lk1_ref, lq2_ref, lk2_ref, *rest, n_pages, lam_init):
    pp = DEC_PAGES
    ckv_refs, kpe_refs, kt_refs, v_refs = rest[:pp], rest[pp:2 * pp], rest[2 * pp:3 * pp], rest[3 * pp:4 * pp]
    oa_ref, ob_ref, lhs_ref, m_a, l_a, acc_a, m_d, l_d, acc_d = rest[4 * pp:]
    j = pl.program_id(1)
    nk = MLA_HEADS * MLA_NOPE
    row = lax.broadcasted_iota(jnp.int32, (MLA_HEADS, 1), 0)
    row_g0 = ((row >> 1) & 1) == 0

    @pl.when(j == 0)
    def _():
        lhs_ref[:nk, :] = wukt_ref[...]
        lhs_ref[nk:, :] = jnp.concatenate(
            [qa_ref[...].astype(BF16), jnp.zeros((lhs_ref.shape[0] - nk - MLA_HEADS, LANES), BF16)], axis=0)
        for r in (m_a, m_d):
            r[...] = jnp.full(r.shape, NEG, F32)
        for r in (l_a, acc_a, l_d, acc_d):
            r[...] = jnp.zeros(r.shape, F32)

    def update(s, m_ref, l_ref):
        m_prev = m_ref[...]
        m_new = jnp.maximum(m_prev, jnp.max(s, axis=-1, keepdims=True))
        alpha = jnp.exp2(m_prev - m_new)
        p = jnp.exp2(s - m_new)
        l_ref[...] = alpha * l_ref[...] + jnp.sum(p, axis=-1, keepdims=True)
        m_ref[...] = m_new
        return alpha, p

    ckv = jnp.concatenate([r[...] for r in ckv_refs], axis=0).astype(BF16)
    res = _dot_nt(lhs_ref[...], ckv)
    sq = res[:nk] * res[:nk]
    ssq = jnp.concatenate(
        [jnp.sum(sq[h * MLA_NOPE:(h + 1) * MLA_NOPE], axis=0, keepdims=True) for h in range(MLA_HEADS)], axis=0)
    rnorm = lax.rsqrt(ssq * (1.0 / MLA_NOPE) + EPS)
    kpe_t = jnp.concatenate([r[...] for r in kpe_refs], axis=1).astype(BF16)
    s = res[nk:nk + MLA_HEADS] * rnorm + _dot(qp_ref[...], kpe_t)
    alpha, p = update(s, m_a, l_a)
    acc_a[...] = alpha * acc_a[...] + _dot(p.astype(BF16), ckv)

    kt = jnp.concatenate([r[...] for r in kt_refs], axis=1).astype(BF16)
    alpha, p = update(_dot(qbd_ref[...], kt), m_d, l_d)
    pb = p.astype(BF16)
    pv = []
    for g in range(DIFF_KV_HEADS):
        v = jnp.concatenate([r[pl.ds(g, PAGE_SIZE, stride=DIFF_KV_HEADS), :] for r in v_refs], axis=0)
        pv.append(_dot(pb, v.astype(BF16)))
    acc_d[...] = alpha * acc_d[...] + jnp.where(row_g0, pv[0], pv[1])

    @pl.when(j == n_pages // pp - 1)
    def _():
        s_self = jnp.sum(qf8_ref[...].astype(F32) * kf8_ref[...].astype(F32), axis=-1, keepdims=True)
        alpha, p = update(s_self, m_a, l_a)
        lat = (alpha * acc_a[...] + p * ckvrow_ref[...]) / l_a[...]
        full = _dot(lat.astype(BF16), wuv_ref[...])
        col_head = lax.broadcasted_iota(jnp.int32, full.shape, 1) >> 6
        row_head = lax.broadcasted_iota(jnp.int32, full.shape, 0)
        oa_ref[...] = jnp.sum(jnp.where(col_head == row_head, full, 0.0), axis=0, keepdims=True)
        s_self = jnp.sum(qbd_ref[...].astype(F32) * dkrow_ref[...].astype(F32), axis=-1, keepdims=True)
        alpha, p = update(s_self, m_d, l_d)
        v_self = jnp.where(row_g0, dvrow_ref[:, :DIFF_VD], dvrow_ref[:, DIFF_VD:])
        o = (alpha * acc_d[...] + p * v_self) / l_d[...]
        lam = _lam(lq1_ref, lk1_ref, lq2_ref, lk2_ref, lam_init)
        d = o[:DIFF_HEADS] - lam * o[DIFF_HEADS:]
        ob_ref[...] = _rms(d, gsub_ref[...]) * (1.0 - lam_init)


def _decode(page_table, per_seq, consts, caches, layer, lam_init):
    b, n_pages = page_table.shape
    pp = DEC_PAGES

    def seq_spec(a):
        return pl.BlockSpec((None,) + a.shape[1:], lambda bb, j, pt: (bb, 0, 0))

    def page_spec(a, k):
        return pl.BlockSpec((None, None) + a.shape[2:], lambda bb, j, pt: (layer, pt[bb, j * pp + k], 0, 0))

    def const_spec(a):
        return pl.BlockSpec(a.shape, lambda bb, j, pt: (0,) * a.ndim)

    in_specs = [seq_spec(a) for a in per_seq] + [const_spec(c) for c in consts]
    args = list(per_seq) + list(consts)
    for c in caches:
        for k in range(pp):
            in_specs.append(page_spec(c, k))
            args.append(c)
    oa_w = MLA_HEADS * MLA_V
    grid_spec = pltpu.PrefetchScalarGridSpec(
        num_scalar_prefetch=1,
        grid=(b, n_pages // pp),
        in_specs=in_specs,
        out_specs=[pl.BlockSpec((None, 1, oa_w), lambda bb, j, pt: (bb, 0, 0)),
                   pl.BlockSpec((None, DIFF_HEADS, DIFF_VD), lambda bb, j, pt: (bb, 0, 0))],
        scratch_shapes=[pltpu.VMEM((MLA_HEADS * MLA_NOPE + 16, LANES), BF16),
                        pltpu.VMEM((MLA_HEADS, 1), F32), pltpu.VMEM((MLA_HEADS, 1), F32),
                        pltpu.VMEM((MLA_HEADS, MLA_KV_RANK), F32),
                        pltpu.VMEM((MLA_HEADS, 1), F32), pltpu.VMEM((MLA_HEADS, 1), F32),
                        pltpu.VMEM((MLA_HEADS, DIFF_VD), F32)],
    )
    return pl.pallas_call(
        functools.partial(_dec_kernel, n_pages=n_pages, lam_init=lam_init),
        grid_spec=grid_spec,
        out_shape=[jax.ShapeDtypeStruct((b, 1, oa_w), F32), jax.ShapeDtypeStruct((b, DIFF_HEADS, DIFF_VD), F32)],
        compiler_params=_cparams(("parallel", "arbitrary")),
        name="decode",
    )(page_table, *args)


def _out_kernel(mix_ref, x_ref, gt_ref, sc_ref, sh_ref, wo_ref, g2_ref, wrh_ref, wrl_ref, br_ref,
                x1_ref, h2_ref, gate_ref):
    o = _dot(mix_ref[...], wo_ref[...])
    x1 = x_ref[...] + gt_ref[...] * o
    x1_ref[...] = x1
    h2 = _rms(x1, g2_ref[...]) * (1.0 + sc_ref[...]) + sh_ref[...]
    h2_ref[...] = h2.astype(BF16)
    hh, hl = _split(h2)
    logits = _dot(hh, wrh_ref[...]) + _dot(hh, wrl_ref[...]) + _dot(hl, wrh_ref[...]) + br_ref[...]
    lane_i = lax.broadcasted_iota(jnp.int32, logits.shape, 1)
    lane = lane_i.astype(F32)
    big = float(ROUTER_LANES)
    gl = jnp.where(lane_i < N_GROUPS, logits, NEG)
    gmax = jnp.max(gl, axis=-1, keepdims=True)
    gidx = jnp.min(jnp.where(gl == gmax, lane, big), axis=-1, keepdims=True)
    g_w = 1.0 / jnp.sum(jnp.exp(gl - gmax), axis=-1, keepdims=True)
    in_group = (lane_i >= N_GROUPS) & (lane_i < N_GROUPS + N_ROUTED) & (
        ((lane_i - N_GROUPS) >> 3).astype(F32) == gidx)
    el = jnp.where(in_group, logits, NEG)
    e1 = jnp.max(el, axis=-1, keepdims=True)
    i1 = jnp.min(jnp.where(el == e1, lane, big), axis=-1, keepdims=True)
    el2 = jnp.where(lane == i1, NEG, el)
    e2 = jnp.max(el2, axis=-1, keepdims=True)
    i2 = jnp.min(jnp.where(el2 == e2, lane, big), axis=-1, keepdims=True)
    t = jnp.exp(e2 - e1)
    w1 = 1.0 / (1.0 + t)
    w2 = t / (1.0 + t)
    gate_ref[...] = jnp.where(lane == i1, w1, jnp.where(lane == i2, w2, 0.0)) * g_w


def _out(mix, x3, gt, sc, sh, wo, g2, wrh, wrl, br, tm):
    b, t, d = x3.shape
    per_tok = gt.shape[1] != 1

    def tok_spec(w):
        return pl.BlockSpec((None, tm, w), lambda s, bb: (bb, s, 0))

    mod_spec = tok_spec(d) if per_tok else pl.BlockSpec((None, 1, d), lambda s, bb: (bb, 0, 0))
    consts = [wo, g2, wrh, wrl, br]
    return pl.pallas_call(
        _out_kernel,
        grid=(t // tm, b),
        in_specs=[tok_spec(mix.shape[2]), tok_spec(d), mod_spec, mod_spec, mod_spec] + [_const_spec(c.shape) for c in consts],
        out_specs=[tok_spec(d), tok_spec(d), tok_spec(ROUTER_LANES)],
        out_shape=[jax.ShapeDtypeStruct((b, t, d), F32), jax.ShapeDtypeStruct((b, t, d), BF16),
                   jax.ShapeDtypeStruct((b, t, ROUTER_LANES), F32)],
        compiler_params=_cparams(("parallel", "parallel")),
        name="out",
    )(mix, x3, gt, sc, sh, *consts)


def _moe_kernel(h2_ref, gate_ref, x1_ref, gt_ref, wg_ref, wu_ref, wd_ref, ex_ref, y_ref, acc_ref):
    g = pl.program_id(2)

    @pl.when(g == 0)
    def _():
        acc_ref[...] = jnp.zeros(acc_ref.shape, F32)

    h = h2_ref[...]
    a = _silu(_dot(h, wg_ref[...])) * _dot(h, wu_ref[...])
    ge = _dot(gate_ref[...].astype(BF16), ex_ref[...])
    acc_ref[...] += _dot((a * ge).astype(BF16), wd_ref[...])

    @pl.when(g == N_GROUPS - 1)
    def _():
        y_ref[...] = x1_ref[...] + gt_ref[...] * acc_ref[...]


def _moe(h2, gate, x1, gt, wg, wu, wd, ex, tm):
    b, t, d = x1.shape
    per_tok = gt.shape[1] != 1

    def tok_spec(w):
        return pl.BlockSpec((None, tm, w), lambda s, bb, g: (bb, s, 0))

    mod_spec = tok_spec(d) if per_tok else pl.BlockSpec((None, 1, d), lambda s, bb, g: (bb, 0, 0))

    def grp_spec(a):
        return pl.BlockSpec((None,) + a.shape[1:], lambda s, bb, g: (g, 0, 0))

    return pl.pallas_call(
        _moe_kernel,
        grid=(t // tm, b, N_GROUPS),
        in_specs=[tok_spec(d), tok_spec(ROUTER_LANES), tok_spec(d), mod_spec,
                  grp_spec(wg), grp_spec(wu), grp_spec(wd), grp_spec(ex)],
        out_specs=tok_spec(d),
        out_shape=jax.ShapeDtypeStruct((b, t, d), F32),
        scratch_shapes=[pltpu.VMEM((tm, d), F32)],
        compiler_params=_cparams(("parallel", "parallel", "arbitrary")),
        name="moe",
    )(h2, gate, x1, gt, wg, wu, wd, ex)


def _rope_tables(pos):
    def cs(dim):
        half = dim // 2
        inv = ROPE_THETA ** (-jnp.arange(half, dtype=F32) * 2.0 / dim)
        ang = pos[:, None] * inv[None, :]
        c, s = jnp.cos(ang), jnp.sin(ang)
        return jnp.concatenate([c, c], axis=1), jnp.concatenate([-s, s], axis=1)

    t = pos.shape[0]
    c32, s32 = cs(MLA_ROPE)
    c64, s64 = cs(DIFF_HD)
    z = lambda w: jnp.zeros((t, w), F32)
    pad = LANES - MLA_NOPE - MLA_ROPE
    cosq = jnp.concatenate([jnp.ones((t, MLA_NOPE), F32), c32, z(pad)], axis=1)
    sinq = jnp.concatenate([z(MLA_NOPE), s32, z(pad)], axis=1)
    cosk = jnp.concatenate([c32, z(LANES - MLA_ROPE)], axis=1)
    sink = jnp.concatenate([s32, z(LANES - MLA_ROPE)], axis=1)
    cosd = jnp.concatenate([c64, c64], axis=1)
    sind = jnp.concatenate([s64, s64], axis=1)
    return [cosq, sinq, cosk, sink, cosd, sind]


def _block_diag_mean(sizes, width):
    m = jnp.zeros((width, width), F32)
    o = 0
    while o < width:
        for sz in sizes:
            if sz > 0:
                m = m.at[o:o + sz, o:o + sz].set(1.0 / sz)
            o += abs(sz)
    return m.astype(BF16)


def _layer_weights(l, w_in, g_norm1, g_mla_qa, w_mla_uq, g_mla_kva, w_mla_uk, g_mla_qn_nope, g_mla_qn_rope,
                   g_mla_kn_nope, g_mla_kn_rope, g_diff_qn, g_diff_kn):
    d = w_in.shape[1]
    o_kpe = MLA_Q_RANK + MLA_KV_RANK
    wi = w_in[l]
    win = jnp.concatenate([wi[:, :o_kpe], wi[:, o_kpe:o_kpe + MLA_ROPE], jnp.zeros((d, LANES - MLA_ROPE), F32),
                           wi[:, o_kpe + MLA_ROPE:]], axis=1).astype(BF16)
    pad = LANES - MLA_NOPE - MLA_ROPE
    wuq = w_mla_uq[l].reshape(MLA_Q_RANK, MLA_HEADS, MLA_NOPE + MLA_ROPE)
    wuq = jnp.concatenate([wuq, jnp.zeros((MLA_Q_RANK, MLA_HEADS, pad), F32)], axis=2).reshape(MLA_Q_RANK, QK_W).astype(BF16)
    wuk = jnp.concatenate([w_mla_uk[l], jnp.zeros((MLA_KV_RANK, MLA_HEADS, LANES - MLA_NOPE), F32)], axis=2)
    wuk = wuk.reshape(MLA_KV_RANK, QK_W).astype(BF16)
    gq = jnp.tile(jnp.concatenate([g_mla_qn_nope[l], g_mla_qn_rope[l], jnp.zeros((pad,), F32)]), MLA_HEADS)[None]
    gk = jnp.tile(jnp.concatenate([g_mla_kn_nope[l], jnp.zeros((LANES - MLA_NOPE,), F32)]), MLA_HEADS)[None]
    gkpe = jnp.concatenate([g_mla_kn_rope[l], jnp.zeros((LANES - MLA_ROPE,), F32)])[None]
    gdq = jnp.tile(g_diff_qn[l], DQ_W // DIFF_HD)[None]
    gdk = jnp.tile(g_diff_kn[l], DK_W // DIFF_HD)[None]
    bdq = _block_diag_mean((MLA_NOPE, MLA_ROPE, -pad), MXU_DIM)
    bdd = _block_diag_mean((DIFF_HD,), MXU_DIM)
    return [g_norm1[l][None], win, g_mla_qa[l][None], wuq, gq, g_mla_kva[l][None], wuk, gk, gkpe, gdq, gdk, bdq, bdd]


def kernel(x_prompt, x_sample, cache_mla_ckv, cache_mla_kpe, cache_diff_k, cache_diff_v, page_table, c_prompt, c_sample, w_ada, b_ada, g_norm1, w_in, g_mla_qa, w_mla_uq, g_mla_kva, w_mla_uk, w_mla_uv, g_mla_qn_nope, g_mla_qn_rope, g_mla_kn_nope, g_mla_kn_rope, g_diff_qn, g_diff_kn, lam_q1, lam_k1, lam_q2, lam_k2, g_diff_subln, w_o, g_norm2, w_router_group, b_router_group, w_router_expert, b_router_expert, w_exp_gate, w_exp_up, w_exp_down):
    bp, sp, d = x_prompt.shape
    bs, ts, _ = x_sample.shape
    depth = w_in.shape[0]
    n_pool = cache_mla_ckv.shape[1]
    n_pages = page_table.shape[1]
    assert ts == 1 and n_pages % DEC_PAGES == 0 and cache_mla_ckv.shape[2] == PAGE_SIZE
    assert w_in.shape[2] == PROJ_W - LANES + MLA_ROPE and d % MXU_DIM == 0
    past = n_pages * PAGE_SIZE

    tm_p = min(256, sp)
    tq = min(512, sp)
    tm_o = min(512, sp)
    tables_p = _rope_tables(jnp.arange(sp, dtype=F32))
    tables_s = _rope_tables(jnp.arange(ts, dtype=F32) + past)

    kpe_t = jnp.transpose(cache_mla_kpe, (0, 1, 3, 2))
    k_t = jnp.transpose(cache_diff_k, (0, 1, 3, 4, 5, 2)).reshape(depth, n_pool, DK_W, PAGE_SIZE)
    v_rows = cache_diff_v.reshape(depth, n_pool, PAGE_SIZE * DIFF_KV_HEADS, DIFF_VD)

    xp = x_prompt
    xs = x_sample.reshape(1, bs, d)
    outs_p = [[], [], [], []]
    outs_s = [[], [], [], []]
    for l in range(depth):
        lam_init = 0.8 - 0.6 * math.exp(-0.3 * l)
        wts = _layer_weights(l, w_in, g_norm1, g_mla_qa, w_mla_uq, g_mla_kva, w_mla_uk, g_mla_qn_nope,
                             g_mla_qn_rope, g_mla_kn_nope, g_mla_kn_rope, g_diff_qn, g_diff_kn)
        lams = [lam_q1[l][None], lam_k1[l][None], lam_q2[l][None], lam_k2[l][None]]
        gsub = g_diff_subln[l][None]
        wuv = w_mla_uv[l]
        wuv_flat = wuv.reshape(MLA_KV_RANK, MLA_HEADS * MLA_V).astype(BF16)
        wuv_bd = jnp.einsum('chd,hg->hcgd', wuv, jnp.eye(MLA_HEADS, dtype=F32)).reshape(
            MLA_HEADS * MLA_KV_RANK, MLA_HEADS * MLA_V).astype(BF16)
        wuk_t = jnp.concatenate([jnp.transpose(w_mla_uk[l], (1, 2, 0)),
                                 jnp.zeros((MLA_HEADS, LANES - MLA_NOPE, MLA_KV_RANK), F32)], axis=1).astype(BF16)
        wukt_rows = jnp.transpose(w_mla_uk[l], (1, 2, 0)).reshape(MLA_HEADS * MLA_NOPE, MLA_KV_RANK).astype(BF16)
        gk_row = jnp.concatenate([g_mla_kn_nope[l], jnp.zeros((LANES - MLA_NOPE,), F32)])[None]
        wo = w_o[l].astype(BF16)
        wr = jnp.concatenate([w_router_group[l], jnp.transpose(w_router_expert[l], (1, 0, 2)).reshape(d, N_ROUTED),
                              jnp.zeros((d, ROUTER_LANES - N_GROUPS - N_ROUTED), F32)], axis=1)
        wrh, wrl = _split(wr)
        br = jnp.concatenate([b_router_group[l], b_router_expert[l].reshape(N_ROUTED),
                              jnp.zeros((ROUTER_LANES - N_GROUPS - N_ROUTED,), F32)])[None]
        wg = jnp.transpose(w_exp_gate[l], (0, 2, 1, 3)).reshape(N_GROUPS, d, GROUP_FF).astype(BF16)
        wu = jnp.transpose(w_exp_up[l], (0, 2, 1, 3)).reshape(N_GROUPS, d, GROUP_FF).astype(BF16)
        wd = w_exp_down[l].reshape(N_GROUPS, GROUP_FF, d).astype(BF16)
        lane = jnp.arange(ROUTER_LANES)[None, :, None]
        col = jnp.arange(GROUP_FF)[None, None, :]
        grp = jnp.arange(N_GROUPS)[:, None, None]
        ex = (lane == N_GROUPS + grp * EXPERTS_PER_GROUP + col // EXPERT_FF).astype(BF16)

        mod = _ada(jnp.concatenate([c_prompt, c_sample], axis=0), w_ada[l], b_ada[l][None])
        mod_p = mod[:bp].reshape(bp, 6, 1, d)
        mod_s = mod[bp:].reshape(1, bs, 6, d)
        sh1p, sc1p, gt1p, sh2p, sc2p, gt2p = [mod_p[:, k] for k in range(6)]
        sh1s, sc1s, gt1s, sh2s, sc2s, gt2s = [mod_s[:, :, k] for k in range(6)]

        (qf, kf, dq0, dq1, ckvb, dkb, dvb, ckv, kpe, dk, dv) = _proj(xp, sc1p, sh1p, tables_p, wts, tm_p)
        mix = _attn(qf, dq0, dq1, kf, ckvb, dkb, dvb, wuv_bd, gsub, lams, lam_init, tq)
        x1, h2, gate = _out(mix, xp, gt1p, sc2p, sh2p, wo, g_norm2[l][None], wrh, wrl, br, tm_o)
        xp = _moe(h2, gate, x1, gt2p, wg, wu, wd, ex, tm_o)
        for lst, a in zip(outs_p, (ckv, kpe, dk, dv)):
            lst.append(a)

        (qf, kf, dq0, dq1, ckvb, dkb, dvb, ckv, kpe, dk, dv) = _proj(xs, sc1s, sh1s, tables_s, wts, bs)
        qf2 = qf.reshape(bs, QK_W)
        qa = _qabs(qf2, gk_row, wuk_t).reshape(bs, MLA_HEADS, LANES)
        qf8 = qf2.reshape(bs, MLA_HEADS, LANES)
        kf8 = kf.reshape(bs, MLA_HEADS, LANES)
        qp = qf8[:, :, MLA_NOPE:MLA_NOPE + MLA_ROPE]
        dq5 = (dq0 + dq1).reshape(bs, DIFF_KV_HEADS, 2, 2, DIFF_HD)
        eye = jnp.eye(2, dtype=BF16)
        qbd = jnp.einsum('bgrmd,gh,mn->bmgrhnd', dq5, eye, eye).reshape(bs, 2 * DIFF_HEADS, DK_W)
        per_seq = [qa, qp, qbd, qf8, kf8, dkb.reshape(bs, 1, DK_W), ckv.reshape(bs, 1, MLA_KV_RANK),
                   dv.reshape(bs, 1, DV_W)]
        consts = [wukt_rows, wuv_flat, gsub] + lams
        oa, ob = _decode(page_table, per_seq, consts, (cache_mla_ckv, kpe_t, k_t, v_rows), l, lam_init)
        mix = jnp.concatenate([oa.reshape(bs, -1), ob.reshape(bs, -1)], axis=1).astype(BF16).reshape(1, bs, -1)
        x1, h2, gate = _out(mix, xs, gt1s, sc2s, sh2s, wo, g_norm2[l][None], wrh, wrl, br, bs)
        xs = _moe(h2, gate, x1, gt2s, wg, wu, wd, ex, bs)
        for lst, a in zip(outs_s, (ckv, kpe, dk, dv)):
            lst.append(a)

    def stack_p(lst, tail):
        return jnp.stack(lst).reshape((depth, bp, sp) + tail)

    def stack_s(lst, tail):
        return jnp.stack(lst).reshape((depth, bs, ts) + tail)

    k_tail = (DIFF_KV_HEADS, 2, DIFF_HD)
    v_tail = (DIFF_KV_HEADS, DIFF_VD)
    return (xp, xs.reshape(bs, ts, d),
            stack_p(outs_p[0], (MLA_KV_RANK,)), stack_p(outs_p[1], (MLA_ROPE,)), stack_p(outs_p[2], k_tail),
            stack_p(outs_p[3], v_tail),
            stack_s(outs_s[0], (MLA_KV_RANK,)), stack_s(outs_s[1], (MLA_ROPE,)), stack_s(outs_s[2], k_tail),
            stack_s(outs_s[3], v_tail))
```

```python
import functools
import math

import jax
import jax.numpy as jnp
from jax import lax
from jax.experimental import pallas as pl
from jax.experimental.pallas import tpu as pltpu

F32 = jnp.float32
BF16 = jnp.bfloat16

MLA_HEADS = 8
MLA_Q_RANK = 256
MLA_KV_RANK = 128
MLA_NOPE = 64
MLA_ROPE = 32
MLA_V = 64
DIFF_HEADS = 4
DIFF_KV_HEADS = 2
DIFF_HD = 64
DIFF_VD = 128
N_GROUPS = 4
EXPERTS_PER_GROUP = 8
EXPERT_FF = 128
PAGE_SIZE = 128
ROPE_THETA = 10000.0
EPS = 1e-6
LOG2E = 1.4426950408889634
MLA_SCALE = (MLA_NOPE + MLA_ROPE) ** -0.5
DIFF_SCALE = DIFF_HD ** -0.5
NEG = -1e30

LANES = 128
MXU_DIM = 256
VMEM_LIMIT = 56 * 1024 * 1024
ROUTER_LANES = 128
N_ROUTED = N_GROUPS * EXPERTS_PER_GROUP
GROUP_FF = EXPERTS_PER_GROUP * EXPERT_FF
QK_W = MLA_HEADS * LANES
DQ_W = DIFF_HEADS * 2 * DIFF_HD
DK_W = DIFF_KV_HEADS * 2 * DIFF_HD
DV_W = DIFF_KV_HEADS * DIFF_VD
PROJ_W = MLA_Q_RANK + MLA_KV_RANK + LANES + DQ_W + DK_W + DV_W
DEC_PAGES = 16
DEC_CHAIN_PAGES = 2
DEC_CHAINS = DEC_PAGES // DEC_CHAIN_PAGES
DEC_SKEW = 2


def _dot(a, b):
    return jnp.dot(a, b, preferred_element_type=F32)


def _dot_nt(a, b):
    return lax.dot_general(a, b, (((1,), (1,)), ((), ())), preferred_element_type=F32)


def _split(a):
    hi = a.astype(BF16)
    lo = (a - hi.astype(F32)).astype(BF16)
    return hi, lo


def _rms(v, g):
    return v * lax.rsqrt(jnp.mean(v * v, axis=-1, keepdims=True) + EPS) * g


def _silu(v):
    return v / (1.0 + jnp.exp(-v))


def _cparams(sem):
    return pltpu.CompilerParams(dimension_semantics=sem, vmem_limit_bytes=VMEM_LIMIT)


def _const_spec(shape):
    nd = len(shape)
    return pl.BlockSpec(shape, lambda *_: (0,) * nd)


def _ada_kernel(c_ref, w_ref, b_ref, o_ref):
    s = _silu(c_ref[...])
    sh, sl = _split(s)
    wh, wl = _split(w_ref[...])
    o_ref[...] = _dot(sh, wh) + _dot(sh, wl) + _dot(sl, wh) + b_ref[...]


def _ada(c, w, b):
    m, d = c.shape
    n = w.shape[1]
    tn = 512
    return pl.pallas_call(
        _ada_kernel,
        grid=(n // tn,),
        in_specs=[_const_spec((m, d)), pl.BlockSpec((d, tn), lambda i: (0, i)), pl.BlockSpec((1, tn), lambda i: (0, i))],
        out_specs=pl.BlockSpec((m, tn), lambda i: (0, i)),
        out_shape=jax.ShapeDtypeStruct((m, n), F32),
        compiler_params=_cparams(("parallel",)),
        name="ada",
    )(c, w, b)


def _block_norm(v, bd, g):
    w = v.shape[1]
    sq = (v * v).astype(BF16)
    ms = jnp.concatenate([_dot(sq[:, i:i + MXU_DIM], bd) for i in range(0, w, MXU_DIM)], axis=1)
    return v * lax.rsqrt(ms + EPS) * g


def _rope(v, cos, sin, half, first):
    parts = []
    for i in range(0, v.shape[1], LANES):
        s = v[:, i:i + LANES]
        rot = jnp.where(first, pltpu.roll(s, LANES - half, 1), pltpu.roll(s, half, 1))
        parts.append(s * cos + rot * sin)
    return parts[0] if len(parts) == 1 else jnp.concatenate(parts, axis=1)


def _proj_kernel(x_ref, sc_ref, sh_ref, cosq_ref, sinq_ref, cosk_ref, sink_ref, cosd_ref, sind_ref,
                 g1_ref, win_ref, gqa_ref, wuq_ref, gq_ref, gkva_ref, wuk_ref, gk_ref, gkpe_ref, gdq_ref, gdk_ref,
                 bdq_ref, bdd_ref,
                 qf_ref, kf_ref, dq0_ref, dq1_ref, ckvt_ref, dkb_ref, dvt_ref, ckv_ref, kpe_ref, dk_ref, dv_ref):
    lane = lax.broadcasted_iota(jnp.int32, (1, LANES), 1)
    x = x_ref[...]
    h = _rms(x, g1_ref[...]) * (1.0 + sc_ref[...]) + sh_ref[...]
    proj = _dot(h.astype(BF16), win_ref[...])
    o_ckv = MLA_Q_RANK
    o_kpe = o_ckv + MLA_KV_RANK
    o_dq = o_kpe + LANES
    o_dk = o_dq + DQ_W
    o_dv = o_dk + DK_W

    cqn = _rms(proj[:, :MLA_Q_RANK], gqa_ref[...])
    q = _dot(cqn.astype(BF16), wuq_ref[...])
    qn = _block_norm(q, bdq_ref[...], gq_ref[...])
    q_first = (lane >= MLA_NOPE) & (lane < MLA_NOPE + MLA_ROPE // 2)
    qf = _rope(qn, cosq_ref[...], sinq_ref[...], MLA_ROPE // 2, q_first)
    qf_ref[...] = (qf * (MLA_SCALE * LOG2E)).astype(BF16)

    ckv = _rms(proj[:, o_ckv:o_kpe], gkva_ref[...])
    ckv_ref[...] = ckv
    ckvb = ckv.astype(BF16)
    ckvt_ref[...] = ckv.T.astype(BF16)
    kr = proj[:, o_kpe:o_dq]
    kn = kr * lax.rsqrt(jnp.sum(kr * kr, axis=-1, keepdims=True) * (1.0 / MLA_ROPE) + EPS) * gkpe_ref[...]
    kpe = _rope(kn, cosk_ref[...], sink_ref[...], MLA_ROPE // 2, lane < MLA_ROPE // 2)
    kpe_ref[...] = kpe[:, :MLA_ROPE]

    kraw = _dot(ckvb, wuk_ref[...])
    knn = _block_norm(kraw, bdq_ref[...], gk_ref[...])
    kpe_at_rope = pltpu.roll(kpe, MLA_NOPE, 1)
    kf_ref[...] = jnp.concatenate(
        [knn[:, i:i + LANES] + kpe_at_rope for i in range(0, QK_W, LANES)], axis=1).astype(BF16)

    d_first = (lane & (DIFF_HD - 1)) < DIFF_HD // 2
    dq = _block_norm(proj[:, o_dq:o_dk], bdd_ref[...], gdq_ref[...])
    dq = _rope(dq, cosd_ref[...], sind_ref[...], DIFF_HD // 2, d_first) * (DIFF_SCALE * LOG2E)
    map0 = (lax.broadcasted_iota(jnp.int32, (1, DQ_W), 1) & (LANES - 1)) < DIFF_HD
    dq0_ref[...] = jnp.where(map0, dq, 0.0).astype(BF16)
    dq1_ref[...] = jnp.where(map0, 0.0, dq).astype(BF16)
    dk = _block_norm(proj[:, o_dk:o_dv], bdd_ref[...], gdk_ref[...])
    dk = _rope(dk, cosd_ref[...], sind_ref[...], DIFF_HD // 2, d_first)
    dk_ref[...] = dk
    dkb_ref[...] = dk.astype(BF16)
    dv = proj[:, o_dv:]
    dv_ref[...] = dv
    dvt_ref[...] = dv.T.astype(BF16)


def _proj(x3, sc, sh, tables, wts, tm):
    b, t, d = x3.shape
    nt = t // tm
    per_tok = sc.shape[1] != 1
    tab_rows = tables[0].shape[0]

    def tok_spec(w):
        return pl.BlockSpec((None, tm, w), lambda s, bb: (bb, s, 0))

    mod_spec = tok_spec(d) if per_tok else pl.BlockSpec((None, 1, d), lambda s, bb: (bb, 0, 0))
    tab_spec = (pl.BlockSpec((tm, LANES), lambda s, bb: (s, 0)) if tab_rows != 1
                else pl.BlockSpec((1, LANES), lambda s, bb: (0, 0)))
    in_specs = [tok_spec(d), mod_spec, mod_spec] + [tab_spec] * 6 + [_const_spec(w.shape) for w in wts]
    outs = [(QK_W, BF16, False), (QK_W, BF16, False), (DQ_W, BF16, False), (DQ_W, BF16, False),
            (MLA_KV_RANK, BF16, True), (DK_W, BF16, False), (DV_W, BF16, True),
            (MLA_KV_RANK, F32, False), (MLA_ROPE, F32, False), (DK_W, F32, False), (DV_W, F32, False)]

    def out_spec(w, tr):
        return pl.BlockSpec((None, w, tm), lambda s, bb: (bb, 0, s)) if tr else tok_spec(w)

    return pl.pallas_call(
        _proj_kernel,
        grid=(nt, b),
        in_specs=in_specs,
        out_specs=[out_spec(w, tr) for w, _, tr in outs],
        out_shape=[jax.ShapeDtypeStruct((b, w, t) if tr else (b, t, w), dt) for w, dt, tr in outs],
        compiler_params=_cparams(("parallel", "parallel")),
        name="proj",
    )(x3, sc, sh, *tables, *wts)


def _lam(lq1_ref, lk1_ref, lq2_ref, lk2_ref, lam_init):
    a = jnp.sum(lq1_ref[...] * lk1_ref[...], axis=-1, keepdims=True)
    b = jnp.sum(lq2_ref[...] * lk2_ref[...], axis=-1, keepdims=True)
    return jnp.exp(a) - jnp.exp(b) + lam_init


ATTN_MAPS = MLA_HEADS + 2 * DIFF_HEADS
ATTN_SLOTS = 2


def _col_tree(x, op):
    parts = [x[c * 64:(c + 1) * 64] for c in range(x.shape[0] // 64)]
    while len(parts) > 1:
        parts = [op(parts[a], parts[a + 1]) for a in range(0, len(parts), 2)]
    return parts[0]


def _attn_kernel(qf_ref, dq0_ref, dq1_ref, kf_ref, ckvt_ref, dk_ref, dvt_ref, wuvt_ref, gsub_ref,
                 lq1_ref, lk1_ref, lq2_ref, lk2_ref, out_ref, m_s, l_s, acc_s, s_scr, *, tq, lam_init):
    i = pl.program_id(1)
    j = pl.program_id(2)

    @pl.when(j == 0)
    def _():
        m_s[...] = jnp.full(m_s.shape, NEG, F32)
        l_s[...] = jnp.zeros(l_s.shape, F32)
        acc_s[...] = jnp.zeros(acc_s.shape, F32)

    maps = []
    for h in range(MLA_HEADS):
        sl = slice(h * LANES, (h + 1) * LANES)
        maps.append((qf_ref, sl, kf_ref, sl, None, h))
    for g in range(DIFF_KV_HEADS):
        gs = slice(g * LANES, (g + 1) * LANES)
        for r in range(DIFF_HEADS // DIFF_KV_HEADS):
            sl = slice((g * 2 + r) * LANES, (g * 2 + r + 1) * LANES)
            maps.append((dq0_ref, sl, dk_ref, gs, gs, MLA_HEADS + (g * 2 + r) * 2))
            maps.append((dq1_ref, sl, dk_ref, gs, gs, MLA_HEADS + (g * 2 + r) * 2 + 1))

    def scores(n, keep):
        q_ref, qs, k_ref, ks, _, _ = maps[n]
        st = _dot_nt(k_ref[:, ks], q_ref[:, qs])
        if keep is not None:
            st = jnp.where(keep, st, NEG)
        s_scr[n % ATTN_SLOTS] = st

    def softmax_pv(n):
        _, _, _, _, vs, idx = maps[n]
        vt = ckvt_ref[...] if vs is None else dvt_ref[vs, :]
        st = s_scr[n % ATTN_SLOTS]
        m_prev = m_s[idx]
        m_new = jnp.maximum(m_prev, jnp.max(_col_tree(st, jnp.maximum), axis=0, keepdims=True))
        alpha = jnp.exp2(m_prev - m_new)
        p = jnp.exp2(st - m_new)
        l_s[idx] = alpha * l_s[idx] + jnp.sum(_col_tree(p, jnp.add), axis=0, keepdims=True)
        acc_s[idx] = alpha * acc_s[idx] + _dot(vt, p.astype(BF16))
        m_s[idx] = m_new

    def step(masked):
        keep = None
        if masked:
            keep = lax.broadcasted_iota(jnp.int32, (tq, tq), 0) <= lax.broadcasted_iota(jnp.int32, (tq, tq), 1)
        scores(0, keep)
        for n in range(ATTN_MAPS):
            if n + 1 < ATTN_MAPS:
                scores(n + 1, keep)
            softmax_pv(n)

    @pl.when(j < i)
    def _():
        step(False)

    @pl.when(j == i)
    def _():
        step(True)
        outs = []
        for h in range(MLA_HEADS):
            lat_t = (acc_s[h] / l_s[h]).astype(BF16)
            outs.append(_dot(wuvt_ref[h], lat_t))
        lam = _lam(lq1_ref, lk1_ref, lq2_ref, lk2_ref, lam_init)
        for gr in range(DIFF_HEADS):
            i0 = MLA_HEADS + 2 * gr
            d = acc_s[i0] / l_s[i0] - lam * (acc_s[i0 + 1] / l_s[i0 + 1])
            d = d * lax.rsqrt(jnp.mean(d * d, axis=0, keepdims=True) + EPS) * gsub_ref[...]
            outs.append(d * (1.0 - lam_init))
        out_ref[...] = jnp.concatenate(outs, axis=0).T.astype(out_ref.dtype)


def _attn(qf, dq0, dq1, kf, ckvt, dkb, dvt, wuvt, gsub_col, lams, lam_init, tq):
    b, s, _ = qf.shape
    nq = s // tq
    assert s % tq == 0 and tq % 64 == 0

    def q_spec(w):
        return pl.BlockSpec((None, tq, w), lambda bb, i, j: (bb, i, 0))

    def k_spec(w):
        return pl.BlockSpec((None, tq, w), lambda bb, i, j: (bb, jnp.minimum(i, j), 0))

    def kt_spec(w):
        return pl.BlockSpec((None, w, tq), lambda bb, i, j: (bb, 0, jnp.minimum(i, j)))

    mix_w = MLA_HEADS * MLA_V + DIFF_HEADS * DIFF_VD
    return pl.pallas_call(
        functools.partial(_attn_kernel, tq=tq, lam_init=lam_init),
        grid=(b, nq, nq),
        in_specs=[q_spec(QK_W), q_spec(DQ_W), q_spec(DQ_W), k_spec(QK_W), kt_spec(MLA_KV_RANK), k_spec(DK_W),
                  kt_spec(DV_W), _const_spec(wuvt.shape), _const_spec(gsub_col.shape)]
                 + [_const_spec(l.shape) for l in lams],
        out_specs=q_spec(mix_w),
        out_shape=jax.ShapeDtypeStruct((b, s, mix_w), BF16),
        scratch_shapes=[pltpu.VMEM((ATTN_MAPS, 1, tq), F32), pltpu.VMEM((ATTN_MAPS, 1, tq), F32),
                        pltpu.VMEM((ATTN_MAPS, DIFF_VD, tq), F32), pltpu.VMEM((ATTN_SLOTS, tq, tq), F32)],
        compiler_params=_cparams(("parallel", "parallel", "arbitrary")),
        name="attn",
    )(qf, dq0, dq1, kf, ckvt, dkb, dvt, wuvt, gsub_col, *lams)


def _qabs_kernel(qf_ref, gk_ref, wt_ref, qa_ref):
    for h in range(MLA_HEADS):
        sl = slice(h * LANES, (h + 1) * LANES)
        qg = (qf_ref[:, sl].astype(F32) * gk_ref[...]).astype(BF16)
        qa_ref[:, sl] = _dot(qg, wt_ref[h])


def _qabs(qf, gk_row, wuk_t):
    n = qf.shape[0]
    return pl.pallas_call(
        _qabs_kernel,
        grid=(1,),
        in_specs=[_const_spec(qf.shape), _const_spec(gk_row.shape), _const_spec(wuk_t.shape)],
        out_specs=_const_spec((n, QK_W)),
        out_shape=jax.ShapeDtypeStruct((n, QK_W), F32),
        compiler_params=_cparams(("arbitrary",)),
        name="qabs",
    )(qf, gk_row, wuk_t)


def _dec_kernel(pt_ref, qa_ref, qp_ref, qbd_ref, qf8_ref, kf8_ref, dkrow_ref, ckvrow_ref, dvrow_ref,
                wukt_ref, wuv_ref, gsub_ref, lq1_ref, lk1_ref, lq2_ref, lk2_ref, *rest, n_pages, lam_init):
    pp = DEC_PAGES
    ckv_refs, kpe_refs, kt_refs, v_refs = rest[:pp], rest[pp:2 * pp], rest[2 * pp:3 * pp], rest[3 * pp:4 * pp]
    oa_ref, ob_ref, lhs_ref, m_a, l_a, acc_a, m_d, l_d, acc_d = rest[4 * pp:]
    j = pl.program_id(1)
    nk = MLA_HEADS * MLA_NOPE
    row = lax.broadcasted_iota(jnp.int32, (MLA_HEADS, 1), 0)
    row_g0 = ((row >> 1) & 1) == 0

    @pl.when(j == 0)
    def _():
        lhs_ref[:nk, :] = wukt_ref[...]
        lhs_ref[nk:, :] = jnp.concatenate(
            [qa_ref[...].astype(BF16), jnp.zeros((lhs_ref.shape[0] - nk - MLA_HEADS, LANES), BF16)], axis=0)
        for r in (m_a, m_d):
            r[...] = jnp.full(r.shape, NEG, F32)
        for r in (l_a, acc_a, l_d, acc_d):
            r[...] = jnp.zeros(r.shape, F32)

    def update(s, m_ref, l_ref, c):
        m_prev = m_ref[c]
        m_new = jnp.maximum(m_prev, jnp.max(s, axis=-1, keepdims=True))
        alpha = jnp.exp2(m_prev - m_new)
        p = jnp.exp2(s - m_new)
        l_ref[c] = alpha * l_ref[c] + jnp.sum(p, axis=-1, keepdims=True)
        m_ref[c] = m_new
        return alpha, p

    def pages(c):
        return range(c * DEC_CHAIN_PAGES, (c + 1) * DEC_CHAIN_PAGES)

    def score_products(c):
        ckv = jnp.concatenate([ckv_refs[k][...] for k in pages(c)], axis=0).astype(BF16)
        res = _dot_nt(lhs_ref[...], ckv)
        kpe_t = jnp.concatenate([kpe_refs[k][...] for k in pages(c)], axis=1).astype(BF16)
        bp = _dot(qp_ref[...], kpe_t)
        kt = jnp.concatenate([kt_refs[k][...] for k in pages(c)], axis=1).astype(BF16)
        return ckv, res, bp, _dot(qbd_ref[...], kt)

    def softmax(c, res, bp, sd):
        sq = res[:nk] * res[:nk]
        ssq = jnp.concatenate(
            [jnp.sum(sq[h * MLA_NOPE:(h + 1) * MLA_NOPE], axis=0, keepdims=True) for h in range(MLA_HEADS)], axis=0)
        rnorm = lax.rsqrt(ssq * (1.0 / MLA_NOPE) + EPS)
        alpha_a, p_a = update(res[nk:nk + MLA_HEADS] * rnorm + bp, m_a, l_a, c)
        alpha_d, p_d = update(sd, m_d, l_d, c)
        return alpha_a, p_a.astype(BF16), alpha_d, p_d.astype(BF16)

    def value_products(c, ckv, alpha_a, p_a, alpha_d, p_d):
        acc_a[c] = alpha_a * acc_a[c] + _dot(p_a, ckv)
        pv = []
        for g in range(DIFF_KV_HEADS):
            v = jnp.concatenate(
                [v_refs[k][pl.ds(g, PAGE_SIZE, stride=DIFF_KV_HEADS), :] for k in pages(c)], axis=0)
            pv.append(_dot(p_d, v.astype(BF16)))
        acc_d[c] = alpha_d * acc_d[c] + jnp.where(row_g0, pv[0], pv[1])

    prods = {}
    for t in range(DEC_CHAINS + DEC_SKEW):
        if t < DEC_CHAINS:
            prods[t] = score_products(t)
        c = t - DEC_SKEW
        if c >= 0:
            ckv, res, bp, sd = prods.pop(c)
            value_products(c, ckv, *softmax(c, res, bp, sd))

    @pl.when(j == n_pages // pp - 1)
    def _():
        def merged(s_self, v_self, m_ref, l_ref, acc_ref):
            m = s_self
            for c in range(DEC_CHAINS):
                m = jnp.maximum(m, m_ref[c])
            p_self = jnp.exp2(s_self - m)
            l = p_self
            acc = p_self * v_self
            for c in range(DEC_CHAINS):
                w = jnp.exp2(m_ref[c] - m)
                l = l + w * l_ref[c]
                acc = acc + w * acc_ref[c]
            return acc / l

        s_self = jnp.sum(qf8_ref[...].astype(F32) * kf8_ref[...].astype(F32), axis=-1, keepdims=True)
        lat = merged(s_self, ckvrow_ref[...], m_a, l_a, acc_a)
        full = _dot(lat.astype(BF16), wuv_ref[...])
        col_head = lax.broadcasted_iota(jnp.int32, full.shape, 1) >> 6
        row_head = lax.broadcasted_iota(jnp.int32, full.shape, 0)
        oa_ref[...] = jnp.sum(jnp.where(col_head == row_head, full, 0.0), axis=0, keepdims=True)
        s_self = jnp.sum(qbd_ref[...].astype(F32) * dkrow_ref[...].astype(F32), axis=-1, keepdims=True)
        v_self = jnp.where(row_g0, dvrow_ref[:, :DIFF_VD], dvrow_ref[:, DIFF_VD:])
        o = merged(s_self, v_self, m_d, l_d, acc_d)
        lam = _lam(lq1_ref, lk1_ref, lq2_ref, lk2_ref, lam_init)
        d = o[:DIFF_HEADS] - lam * o[DIFF_HEADS:]
        ob_ref[...] = _rms(d, gsub_ref[...]) * (1.0 - lam_init)


def _decode(page_table, per_seq, consts, caches, layer, lam_init):
    b, n_pages = page_table.shape
    pp = DEC_PAGES

    def seq_spec(a):
        return pl.BlockSpec((None,) + a.shape[1:], lambda bb, j, pt: (bb, 0, 0))

    def page_spec(a, k):
        return pl.BlockSpec((None, None) + a.shape[2:], lambda bb, j, pt: (layer, pt[bb, j * pp + k], 0, 0))

    def const_spec(a):
        return pl.BlockSpec(a.shape, lambda bb, j, pt: (0,) * a.ndim)

    in_specs = [seq_spec(a) for a in per_seq] + [const_spec(c) for c in consts]
    args = list(per_seq) + list(consts)
    for c in caches:
        for k in range(pp):
            in_specs.append(page_spec(c, k))
            args.append(c)
    oa_w = MLA_HEADS * MLA_V
    grid_spec = pltpu.PrefetchScalarGridSpec(
        num_scalar_prefetch=1,
        grid=(b, n_pages // pp),
        in_specs=in_specs,
        out_specs=[pl.BlockSpec((None, 1, oa_w), lambda bb, j, pt: (bb, 0, 0)),
                   pl.BlockSpec((None, DIFF_HEADS, DIFF_VD), lambda bb, j, pt: (bb, 0, 0))],
        scratch_shapes=[pltpu.VMEM((MLA_HEADS * MLA_NOPE + 16, LANES), BF16),
                        pltpu.VMEM((DEC_CHAINS, MLA_HEADS, 1), F32), pltpu.VMEM((DEC_CHAINS, MLA_HEADS, 1), F32),
                        pltpu.VMEM((DEC_CHAINS, MLA_HEADS, MLA_KV_RANK), F32),
                        pltpu.VMEM((DEC_CHAINS, MLA_HEADS, 1), F32), pltpu.VMEM((DEC_CHAINS, MLA_HEADS, 1), F32),
                        pltpu.VMEM((DEC_CHAINS, MLA_HEADS, DIFF_VD), F32)],
    )
    return pl.pallas_call(
        functools.partial(_dec_kernel, n_pages=n_pages, lam_init=lam_init),
        grid_spec=grid_spec,
        out_shape=[jax.ShapeDtypeStruct((b, 1, oa_w), F32), jax.ShapeDtypeStruct((b, DIFF_HEADS, DIFF_VD), F32)],
        compiler_params=_cparams(("parallel", "arbitrary")),
        name="decode",
    )(page_table, *args)


def _out_kernel(mix_ref, x_ref, gt_ref, sc_ref, sh_ref, wo_ref, g2_ref, wrh_ref, wrl_ref, br_ref,
                x1_ref, h2_ref, gate_ref):
    o = _dot(mix_ref[...], wo_ref[...])
    x1 = x_ref[...] + gt_ref[...] * o
    x1_ref[...] = x1
    h2 = _rms(x1, g2_ref[...]) * (1.0 + sc_ref[...]) + sh_ref[...]
    h2_ref[...] = h2.astype(BF16)
    hh, hl = _split(h2)
    logits = _dot(hh, wrh_ref[...]) + _dot(hh, wrl_ref[...]) + _dot(hl, wrh_ref[...]) + br_ref[...]
    lane_i = lax.broadcasted_iota(jnp.int32, logits.shape, 1)
    lane = lane_i.astype(F32)
    big = float(ROUTER_LANES)
    gl = jnp.where(lane_i < N_GROUPS, logits, NEG)
    gmax = jnp.max(gl, axis=-1, keepdims=True)
    gidx = jnp.min(jnp.where(gl == gmax, lane, big), axis=-1, keepdims=True)
    g_w = 1.0 / jnp.sum(jnp.exp(gl - gmax), axis=-1, keepdims=True)
    in_group = (lane_i >= N_GROUPS) & (lane_i < N_GROUPS + N_ROUTED) & (
        ((lane_i - N_GROUPS) >> 3).astype(F32) == gidx)
    el = jnp.where(in_group, logits, NEG)
    e1 = jnp.max(el, axis=-1, keepdims=True)
    i1 = jnp.min(jnp.where(el == e1, lane, big), axis=-1, keepdims=True)
    el2 = jnp.where(lane == i1, NEG, el)
    e2 = jnp.max(el2, axis=-1, keepdims=True)
    i2 = jnp.min(jnp.where(el2 == e2, lane, big), axis=-1, keepdims=True)
    t = jnp.exp(e2 - e1)
    w1 = 1.0 / (1.0 + t)
    w2 = t / (1.0 + t)
    gate_ref[...] = jnp.where(lane == i1, w1, jnp.where(lane == i2, w2, 0.0)) * g_w


def _out(mix, x3, gt, sc, sh, wo, g2, wrh, wrl, br, tm):
    b, t, d = x3.shape
    per_tok = gt.shape[1] != 1

    def tok_spec(w):
        return pl.BlockSpec((None, tm, w), lambda s, bb: (bb, s, 0))

    mod_spec = tok_spec(d) if per_tok else pl.BlockSpec((None, 1, d), lambda s, bb: (bb, 0, 0))
    consts = [wo, g2, wrh, wrl, br]
    return pl.pallas_call(
        _out_kernel,
        grid=(t // tm, b),
        in_specs=[tok_spec(mix.shape[2]), tok_spec(d), mod_spec, mod_spec, mod_spec] + [_const_spec(c.shape) for c in consts],
        out_specs=[tok_spec(d), tok_spec(d), tok_spec(ROUTER_LANES)],
        out_shape=[jax.ShapeDtypeStruct((b, t, d), F32), jax.ShapeDtypeStruct((b, t, d), BF16),
                   jax.ShapeDtypeStruct((b, t, ROUTER_LANES), F32)],
        compiler_params=_cparams(("parallel", "parallel")),
        name="out",
    )(mix, x3, gt, sc, sh, *consts)


def _moe_kernel(h2_ref, gate_ref, x1_ref, gt_ref, wg_ref, wu_ref, wd_ref, ex_ref, y_ref, acc_ref):
    g = pl.program_id(2)

    @pl.when(g == 0)
    def _():
        acc_ref[...] = jnp.zeros(acc_ref.shape, F32)

    h = h2_ref[...]
    a = _silu(_dot(h, wg_ref[...])) * _dot(h, wu_ref[...])
    ge = _dot(gate_ref[...].astype(BF16), ex_ref[...])
    acc_ref[...] += _dot((a * ge).astype(BF16), wd_ref[...])

    @pl.when(g == N_GROUPS - 1)
    def _():
        y_ref[...] = x1_ref[...] + gt_ref[...] * acc_ref[...]


def _moe(h2, gate, x1, gt, wg, wu, wd, ex, tm):
    b, t, d = x1.shape
    per_tok = gt.shape[1] != 1

    def tok_spec(w):
        return pl.BlockSpec((None, tm, w), lambda s, bb, g: (bb, s, 0))

    mod_spec = tok_spec(d) if per_tok else pl.BlockSpec((None, 1, d), lambda s, bb, g: (bb, 0, 0))

    def grp_spec(a):
        return pl.BlockSpec((None,) + a.shape[1:], lambda s, bb, g: (g, 0, 0))

    return pl.pallas_call(
        _moe_kernel,
        grid=(t // tm, b, N_GROUPS),
        in_specs=[tok_spec(d), tok_spec(ROUTER_LANES), tok_spec(d), mod_spec,
                  grp_spec(wg), grp_spec(wu), grp_spec(wd), grp_spec(ex)],
        out_specs=tok_spec(d),
        out_shape=jax.ShapeDtypeStruct((b, t, d), F32),
        scratch_shapes=[pltpu.VMEM((tm, d), F32)],
        compiler_params=_cparams(("parallel", "parallel", "arbitrary")),
        name="moe",
    )(h2, gate, x1, gt, wg, wu, wd, ex)


def _rope_tables(pos):
    def cs(dim):
        half = dim // 2
        inv = ROPE_THETA ** (-jnp.arange(half, dtype=F32) * 2.0 / dim)
        ang = pos[:, None] * inv[None, :]
        c, s = jnp.cos(ang), jnp.sin(ang)
        return jnp.concatenate([c, c], axis=1), jnp.concatenate([-s, s], axis=1)

    t = pos.shape[0]
    c32, s32 = cs(MLA_ROPE)
    c64, s64 = cs(DIFF_HD)
    z = lambda w: jnp.zeros((t, w), F32)
    pad = LANES - MLA_NOPE - MLA_ROPE
    cosq = jnp.concatenate([jnp.ones((t, MLA_NOPE), F32), c32, z(pad)], axis=1)
    sinq = jnp.concatenate([z(MLA_NOPE), s32, z(pad)], axis=1)
    cosk = jnp.concatenate([c32, z(LANES - MLA_ROPE)], axis=1)
    sink = jnp.concatenate([s32, z(LANES - MLA_ROPE)], axis=1)
    cosd = jnp.concatenate([c64, c64], axis=1)
    sind = jnp.concatenate([s64, s64], axis=1)
    return [cosq, sinq, cosk, sink, cosd, sind]


def _block_diag_mean(sizes, width):
    m = jnp.zeros((width, width), F32)
    o = 0
    while o < width:
        for sz in sizes:
            if sz > 0:
                m = m.at[o:o + sz, o:o + sz].set(1.0 / sz)
            o += abs(sz)
    return m.astype(BF16)


def _layer_weights(l, w_in, g_norm1, g_mla_qa, w_mla_uq, g_mla_kva, w_mla_uk, g_mla_qn_nope, g_mla_qn_rope,
                   g_mla_kn_nope, g_mla_kn_rope, g_diff_qn, g_diff_kn):
    d = w_in.shape[1]
    o_kpe = MLA_Q_RANK + MLA_KV_RANK
    wi = w_in[l]
    win = jnp.concatenate([wi[:, :o_kpe], wi[:, o_kpe:o_kpe + MLA_ROPE], jnp.zeros((d, LANES - MLA_ROPE), F32),
                           wi[:, o_kpe + MLA_ROPE:]], axis=1).astype(BF16)
    pad = LANES - MLA_NOPE - MLA_ROPE
    wuq = w_mla_uq[l].reshape(MLA_Q_RANK, MLA_HEADS, MLA_NOPE + MLA_ROPE)
    wuq = jnp.concatenate([wuq, jnp.zeros((MLA_Q_RANK, MLA_HEADS, pad), F32)], axis=2).reshape(MLA_Q_RANK, QK_W).astype(BF16)
    wuk = jnp.concatenate([w_mla_uk[l], jnp.zeros((MLA_KV_RANK, MLA_HEADS, LANES - MLA_NOPE), F32)], axis=2)
    wuk = wuk.reshape(MLA_KV_RANK, QK_W).astype(BF16)
    gq = jnp.tile(jnp.concatenate([g_mla_qn_nope[l], g_mla_qn_rope[l], jnp.zeros((pad,), F32)]), MLA_HEADS)[None]
    gk = jnp.tile(jnp.concatenate([g_mla_kn_nope[l], jnp.zeros((LANES - MLA_NOPE,), F32)]), MLA_HEADS)[None]
    gkpe = jnp.concatenate([g_mla_kn_rope[l], jnp.zeros((LANES - MLA_ROPE,), F32)])[None]
    gdq = jnp.tile(g_diff_qn[l], DQ_W // DIFF_HD)[None]
    gdk = jnp.tile(g_diff_kn[l], DK_W // DIFF_HD)[None]
    bdq = _block_diag_mean((MLA_NOPE, MLA_ROPE, -pad), MXU_DIM)
    bdd = _block_diag_mean((DIFF_HD,), MXU_DIM)
    return [g_norm1[l][None], win, g_mla_qa[l][None], wuq, gq, g_mla_kva[l][None], wuk, gk, gkpe, gdq, gdk, bdq, bdd]


def kernel(x_prompt, x_sample, cache_mla_ckv, cache_mla_kpe, cache_diff_k, cache_diff_v, page_table, c_prompt, c_sample, w_ada, b_ada, g_norm1, w_in, g_mla_qa, w_mla_uq, g_mla_kva, w_mla_uk, w_mla_uv, g_mla_qn_nope, g_mla_qn_rope, g_mla_kn_nope, g_mla_kn_rope, g_diff_qn, g_diff_kn, lam_q1, lam_k1, lam_q2, lam_k2, g_diff_subln, w_o, g_norm2, w_router_group, b_router_group, w_router_expert, b_router_expert, w_exp_gate, w_exp_up, w_exp_down):
    bp, sp, d = x_prompt.shape
    bs, ts, _ = x_sample.shape
    depth = w_in.shape[0]
    n_pool = cache_mla_ckv.shape[1]
    n_pages = page_table.shape[1]
    assert ts == 1 and n_pages % DEC_PAGES == 0 and cache_mla_ckv.shape[2] == PAGE_SIZE
    assert w_in.shape[2] == PROJ_W - LANES + MLA_ROPE and d % MXU_DIM == 0
    past = n_pages * PAGE_SIZE

    tm_p = min(256, sp)
    tq = min(512, sp)
    tm_o = min(512, sp)
    tables_p = _rope_tables(jnp.arange(sp, dtype=F32))
    tables_s = _rope_tables(jnp.arange(ts, dtype=F32) + past)

    kpe_t = jnp.transpose(cache_mla_kpe, (0, 1, 3, 2))
    k_t = jnp.transpose(cache_diff_k, (0, 1, 3, 4, 5, 2)).reshape(depth, n_pool, DK_W, PAGE_SIZE)
    v_rows = cache_diff_v.reshape(depth, n_pool, PAGE_SIZE * DIFF_KV_HEADS, DIFF_VD)

    xp = x_prompt
    xs = x_sample.reshape(1, bs, d)
    outs_p = [[], [], [], []]
    outs_s = [[], [], [], []]
    for l in range(depth):
        lam_init = 0.8 - 0.6 * math.exp(-0.3 * l)
        wts = _layer_weights(l, w_in, g_norm1, g_mla_qa, w_mla_uq, g_mla_kva, w_mla_uk, g_mla_qn_nope,
                             g_mla_qn_rope, g_mla_kn_nope, g_mla_kn_rope, g_diff_qn, g_diff_kn)
        lams = [lam_q1[l][None], lam_k1[l][None], lam_q2[l][None], lam_k2[l][None]]
        gsub = g_diff_subln[l][None]
        wuv = w_mla_uv[l]
        wuv_flat = wuv.reshape(MLA_KV_RANK, MLA_HEADS * MLA_V).astype(BF16)
        wuvt = jnp.transpose(wuv, (1, 2, 0)).astype(BF16)
        wuk_t = jnp.concatenate([jnp.transpose(w_mla_uk[l], (1, 2, 0)),
                                 jnp.zeros((MLA_HEADS, LANES - MLA_NOPE, MLA_KV_RANK), F32)], axis=1).astype(BF16)
        wukt_rows = jnp.transpose(w_mla_uk[l], (1, 2, 0)).reshape(MLA_HEADS * MLA_NOPE, MLA_KV_RANK).astype(BF16)
        gk_row = jnp.concatenate([g_mla_kn_nope[l], jnp.zeros((LANES - MLA_NOPE,), F32)])[None]
        wo = w_o[l].astype(BF16)
        wr = jnp.concatenate([w_router_group[l], jnp.transpose(w_router_expert[l], (1, 0, 2)).reshape(d, N_ROUTED),
                              jnp.zeros((d, ROUTER_LANES - N_GROUPS - N_ROUTED), F32)], axis=1)
        wrh, wrl = _split(wr)
        br = jnp.concatenate([b_router_group[l], b_router_expert[l].reshape(N_ROUTED),
                              jnp.zeros((ROUTER_LANES - N_GROUPS - N_ROUTED,), F32)])[None]
        wg = jnp.transpose(w_exp_gate[l], (0, 2, 1, 3)).reshape(N_GROUPS, d, GROUP_FF).astype(BF16)
        wu = jnp.transpose(w_exp_up[l], (0, 2, 1, 3)).reshape(N_GROUPS, d, GROUP_FF).astype(BF16)
        wd = w_exp_down[l].reshape(N_GROUPS, GROUP_FF, d).astype(BF16)
        lane = jnp.arange(ROUTER_LANES)[None, :, None]
        col = jnp.arange(GROUP_FF)[None, None, :]
        grp = jnp.arange(N_GROUPS)[:, None, None]
        ex = (lane == N_GROUPS + grp * EXPERTS_PER_GROUP + col // EXPERT_FF).astype(BF16)

        mod = _ada(jnp.concatenate([c_prompt, c_sample], axis=0), w_ada[l], b_ada[l][None])
        mod_p = mod[:bp].reshape(bp, 6, 1, d)
        mod_s = mod[bp:].reshape(1, bs, 6, d)
        sh1p, sc1p, gt1p, sh2p, sc2p, gt2p = [mod_p[:, k] for k in range(6)]
        sh1s, sc1s, gt1s, sh2s, sc2s, gt2s = [mod_s[:, :, k] for k in range(6)]

        (qf, kf, dq0, dq1, ckvt, dkb, dvt, ckv, kpe, dk, dv) = _proj(xp, sc1p, sh1p, tables_p, wts, tm_p)
        mix = _attn(qf, dq0, dq1, kf, ckvt, dkb, dvt, wuvt, gsub.reshape(DIFF_VD, 1), lams, lam_init, tq)
        x1, h2, gate = _out(mix, xp, gt1p, sc2p, sh2p, wo, g_norm2[l][None], wrh, wrl, br, tm_o)
        xp = _moe(h2, gate, x1, gt2p, wg, wu, wd, ex, tm_o)
        for lst, a in zip(outs_p, (ckv, kpe, dk, dv)):
            lst.append(a)

        (qf, kf, dq0, dq1, _, dkb, _, ckv, kpe, dk, dv) = _proj(xs, sc1s, sh1s, tables_s, wts, bs)
        qf2 = qf.reshape(bs, QK_W)
        qa = _qabs(qf2, gk_row, wuk_t).reshape(bs, MLA_HEADS, LANES)
        qf8 = qf2.reshape(bs, MLA_HEADS, LANES)
        kf8 = kf.reshape(bs, MLA_HEADS, LANES)
        qp = qf8[:, :, MLA_NOPE:MLA_NOPE + MLA_ROPE]
        dq5 = (dq0 + dq1).reshape(bs, DIFF_KV_HEADS, 2, 2, DIFF_HD)
        eye = jnp.eye(2, dtype=BF16)
        qbd = jnp.einsum('bgrmd,gh,mn->bmgrhnd', dq5, eye, eye).reshape(bs, 2 * DIFF_HEADS, DK_W)
        per_seq = [qa, qp, qbd, qf8, kf8, dkb.reshape(bs, 1, DK_W), ckv.reshape(bs, 1, MLA_KV_RANK),
                   dv.reshape(bs, 1, DV_W)]
        consts = [wukt_rows, wuv_flat, gsub] + lams
        oa, ob = _decode(page_table, per_seq, consts, (cache_mla_ckv, kpe_t, k_t, v_rows), l, lam_init)
        mix = jnp.concatenate([oa.reshape(bs, -1), ob.reshape(bs, -1)], axis=1).astype(BF16).reshape(1, bs, -1)
        x1, h2, gate = _out(mix, xs, gt1s, sc2s, sh2s, wo, g_norm2[l][None], wrh, wrl, br, bs)
        xs = _moe(h2, gate, x1, gt2s, wg, wu, wd, ex, bs)
        for lst, a in zip(outs_s, (ckv, kpe, dk, dv)):
            lst.append(a)

    def stack_p(lst, tail):
        return jnp.stack(lst).reshape((depth, bp, sp) + tail)

    def stack_s(lst, tail):
        return jnp.stack(lst).reshape((depth, bs, ts) + tail)

    k_tail = (DIFF_KV_HEADS, 2, DIFF_HD)
    v_tail = (DIFF_KV_HEADS, DIFF_VD)
    return (xp, xs.reshape(bs, ts, d),
            stack_p(outs_p[0], (MLA_KV_RANK,)), stack_p(outs_p[1], (MLA_ROPE,)), stack_p(outs_p[2], k_tail),
            stack_p(outs_p[3], v_tail),
            stack_s(outs_s[0], (MLA_KV_RANK,)), stack_s(outs_s[1], (MLA_ROPE,)), stack_s(outs_s[2], k_tail),
            stack_s(outs_s[3], v_tail))
```

```python
import functools
import math

import jax
import jax.numpy as jnp
from jax import lax
from jax.experimental import pallas as pl
from jax.experimental.pallas import tpu as pltpu

F32 = jnp.float32
BF16 = jnp.bfloat16

MLA_HEADS = 8
MLA_Q_RANK = 256
MLA_KV_RANK = 128
MLA_NOPE = 64
MLA_ROPE = 32
MLA_V = 64
DIFF_HEADS = 4
DIFF_KV_HEADS = 2
DIFF_HD = 64
DIFF_VD = 128
N_GROUPS = 4
EXPERTS_PER_GROUP = 8
EXPERT_FF = 128
PAGE_SIZE = 128
ROPE_THETA = 10000.0
EPS = 1e-6
LOG2E = 1.4426950408889634
MLA_SCALE = (MLA_NOPE + MLA_ROPE) ** -0.5
DIFF_SCALE = DIFF_HD ** -0.5
NEG = -1e30

LANES = 128
MXU_DIM = 256
VMEM_LIMIT = 56 * 1024 * 1024
ROUTER_LANES = 128
N_ROUTED = N_GROUPS * EXPERTS_PER_GROUP
GROUP_FF = EXPERTS_PER_GROUP * EXPERT_FF
QK_W = MLA_HEADS * LANES
DQ_W = DIFF_HEADS * 2 * DIFF_HD
DK_W = DIFF_KV_HEADS * 2 * DIFF_HD
DV_W = DIFF_KV_HEADS * DIFF_VD
PROJ_W = MLA_Q_RANK + MLA_KV_RANK + LANES + DQ_W + DK_W + DV_W
DEC_PAGES = 16
DEC_CHAIN_PAGES = 2
DEC_CHAINS = DEC_PAGES // DEC_CHAIN_PAGES
DEC_SKEW = 2


def _dot(a, b):
    return jnp.dot(a, b, preferred_element_type=F32)


def _dot_nt(a, b):
    return lax.dot_general(a, b, (((1,), (1,)), ((), ())), preferred_element_type=F32)


def _split(a):
    hi = a.astype(BF16)
    lo = (a - hi.astype(F32)).astype(BF16)
    return hi, lo


def _rms(v, g):
    return v * lax.rsqrt(jnp.mean(v * v, axis=-1, keepdims=True) + EPS) * g


def _silu(v):
    return v / (1.0 + jnp.exp(-v))


def _cparams(sem):
    return pltpu.CompilerParams(dimension_semantics=sem, vmem_limit_bytes=VMEM_LIMIT)


def _const_spec(shape):
    nd = len(shape)
    return pl.BlockSpec(shape, lambda *_: (0,) * nd)


def _ada_kernel(c_ref, w_ref, b_ref, o_ref):
    s = _silu(c_ref[...])
    sh, sl = _split(s)
    wh, wl = _split(w_ref[...])
    o_ref[...] = _dot(sh, wh) + _dot(sh, wl) + _dot(sl, wh) + b_ref[...]


def _ada(c, w, b):
    m, d = c.shape
    n = w.shape[1]
    tn = 512
    return pl.pallas_call(
        _ada_kernel,
        grid=(n // tn,),
        in_specs=[_const_spec((m, d)), pl.BlockSpec((d, tn), lambda i: (0, i)), pl.BlockSpec((1, tn), lambda i: (0, i))],
        out_specs=pl.BlockSpec((m, tn), lambda i: (0, i)),
        out_shape=jax.ShapeDtypeStruct((m, n), F32),
        compiler_params=_cparams(("parallel",)),
        name="ada",
    )(c, w, b)


def _block_norm(v, bd, g):
    w = v.shape[1]
    sq = (v * v).astype(BF16)
    ms = jnp.concatenate([_dot(sq[:, i:i + MXU_DIM], bd) for i in range(0, w, MXU_DIM)], axis=1)
    return v * lax.rsqrt(ms + EPS) * g


def _rope(v, cos, sin, half, first):
    parts = []
    for i in range(0, v.shape[1], LANES):
        s = v[:, i:i + LANES]
        rot = jnp.where(first, pltpu.roll(s, LANES - half, 1), pltpu.roll(s, half, 1))
        parts.append(s * cos + rot * sin)
    return parts[0] if len(parts) == 1 else jnp.concatenate(parts, axis=1)


def _proj_kernel(x_ref, sc_ref, sh_ref, cosq_ref, sinq_ref, cosk_ref, sink_ref, cosd_ref, sind_ref,
                 g1_ref, win_ref, gqa_ref, wuq_ref, gq_ref, gkva_ref, wuk_ref, gk_ref, gkpe_ref, gdq_ref, gdk_ref,
                 bdq_ref, bdd_ref,
                 qf_ref, kf_ref, dq0_ref, dq1_ref, ckvt_ref, dkb_ref, dvt_ref, ckv_ref, kpe_ref, dk_ref, dv_ref):
    lane = lax.broadcasted_iota(jnp.int32, (1, LANES), 1)
    x = x_ref[...]
    h = _rms(x, g1_ref[...]) * (1.0 + sc_ref[...]) + sh_ref[...]
    proj = _dot(h.astype(BF16), win_ref[...])
    o_ckv = MLA_Q_RANK
    o_kpe = o_ckv + MLA_KV_RANK
    o_dq = o_kpe + LANES
    o_dk = o_dq + DQ_W
    o_dv = o_dk + DK_W

    cqn = _rms(proj[:, :MLA_Q_RANK], gqa_ref[...])
    q = _dot(cqn.astype(BF16), wuq_ref[...])
    qn = _block_norm(q, bdq_ref[...], gq_ref[...])
    q_first = (lane >= MLA_NOPE) & (lane < MLA_NOPE + MLA_ROPE // 2)
    qf = _rope(qn, cosq_ref[...], sinq_ref[...], MLA_ROPE // 2, q_first)
    qf_ref[...] = (qf * (MLA_SCALE * LOG2E)).astype(BF16)

    ckv = _rms(proj[:, o_ckv:o_kpe], gkva_ref[...])
    ckv_ref[...] = ckv
    ckvb = ckv.astype(BF16)
    ckvt_ref[...] = ckv.T.astype(BF16)
    kr = proj[:, o_kpe:o_dq]
    kn = kr * lax.rsqrt(jnp.sum(kr * kr, axis=-1, keepdims=True) * (1.0 / MLA_ROPE) + EPS) * gkpe_ref[...]
    kpe = _rope(kn, cosk_ref[...], sink_ref[...], MLA_ROPE // 2, lane < MLA_ROPE // 2)
    kpe_ref[...] = kpe[:, :MLA_ROPE]

    kraw = _dot(ckvb, wuk_ref[...])
    knn = _block_norm(kraw, bdq_ref[...], gk_ref[...])
    kpe_at_rope = pltpu.roll(kpe, MLA_NOPE, 1)
    kf_ref[...] = jnp.concatenate(
        [knn[:, i:i + LANES] + kpe_at_rope for i in range(0, QK_W, LANES)], axis=1).astype(BF16)

    d_first = (lane & (DIFF_HD - 1)) < DIFF_HD // 2
    dq = _block_norm(proj[:, o_dq:o_dk], bdd_ref[...], gdq_ref[...])
    dq = _rope(dq, cosd_ref[...], sind_ref[...], DIFF_HD // 2, d_first) * (DIFF_SCALE * LOG2E)
    map0 = (lax.broadcasted_iota(jnp.int32, (1, DQ_W), 1) & (LANES - 1)) < DIFF_HD
    dq0_ref[...] = jnp.where(map0, dq, 0.0).astype(BF16)
    dq1_ref[...] = jnp.where(map0, 0.0, dq).astype(BF16)
    dk = _block_norm(proj[:, o_dk:o_dv], bdd_ref[...], gdk_ref[...])
    dk = _rope(dk, cosd_ref[...], sind_ref[...], DIFF_HD // 2, d_first)
    dk_ref[...] = dk
    dkb_ref[...] = dk.astype(BF16)
    dv = proj[:, o_dv:]
    dv_ref[...] = dv
    dvt_ref[...] = dv.T.astype(BF16)


def _proj(x3, sc, sh, tables, wts, tm):
    b, t, d = x3.shape
    nt = t // tm
    per_tok = sc.shape[1] != 1
    tab_rows = tables[0].shape[0]

    def tok_spec(w):
        return pl.BlockSpec((None, tm, w), lambda s, bb: (bb, s, 0))

    mod_spec = tok_spec(d) if per_tok else pl.BlockSpec((None, 1, d), lambda s, bb: (bb, 0, 0))
    tab_spec = (pl.BlockSpec((tm, LANES), lambda s, bb: (s, 0)) if tab_rows != 1
                else pl.BlockSpec((1, LANES), lambda s, bb: (0, 0)))
    in_specs = [tok_spec(d), mod_spec, mod_spec] + [tab_spec] * 6 + [_const_spec(w.shape) for w in wts]
    outs = [(QK_W, BF16, False), (QK_W, BF16, False), (DQ_W, BF16, False), (DQ_W, BF16, False),
            (MLA_KV_RANK, BF16, True), (DK_W, BF16, False), (DV_W, BF16, True),
            (MLA_KV_RANK, F32, False), (MLA_ROPE, F32, False), (DK_W, F32, False), (DV_W, F32, False)]

    def out_spec(w, tr):
        return pl.BlockSpec((None, w, tm), lambda s, bb: (bb, 0, s)) if tr else tok_spec(w)

    return pl.pallas_call(
        _proj_kernel,
        grid=(nt, b),
        in_specs=in_specs,
        out_specs=[out_spec(w, tr) for w, _, tr in outs],
        out_shape=[jax.ShapeDtypeStruct((b, w, t) if tr else (b, t, w), dt) for w, dt, tr in outs],
        compiler_params=_cparams(("parallel", "parallel")),
        name="proj",
    )(x3, sc, sh, *tables, *wts)


def _lam(lq1_ref, lk1_ref, lq2_ref, lk2_ref, lam_init):
    a = jnp.sum(lq1_ref[...] * lk1_ref[...], axis=-1, keepdims=True)
    b = jnp.sum(lq2_ref[...] * lk2_ref[...], axis=-1, keepdims=True)
    return jnp.exp(a) - jnp.exp(b) + lam_init


ATTN_MAPS = MLA_HEADS + 2 * DIFF_HEADS
ATTN_SLOTS = 2


def _col_tree(x, op):
    parts = [x[c * 64:(c + 1) * 64] for c in range(x.shape[0] // 64)]
    while len(parts) > 1:
        parts = [op(parts[a], parts[a + 1]) for a in range(0, len(parts), 2)]
    return parts[0]


def _attn_kernel(qf_ref, dq0_ref, dq1_ref, kf_ref, ckvt_ref, dk_ref, dvt_ref, wuvt_ref, gsub_ref,
                 lq1_ref, lk1_ref, lq2_ref, lk2_ref, out_ref, m_s, l_s, acc_s, s_scr, *, tq, lam_init):
    i = pl.program_id(1)
    j = pl.program_id(2)

    @pl.when(j == 0)
    def _():
        m_s[...] = jnp.full(m_s.shape, NEG, F32)
        l_s[...] = jnp.zeros(l_s.shape, F32)
        acc_s[...] = jnp.zeros(acc_s.shape, F32)

    maps = []
    for h in range(MLA_HEADS):
        sl = slice(h * LANES, (h + 1) * LANES)
        maps.append((qf_ref, sl, kf_ref, sl, None, h))
    for g in range(DIFF_KV_HEADS):
        gs = slice(g * LANES, (g + 1) * LANES)
        for r in range(DIFF_HEADS // DIFF_KV_HEADS):
            sl = slice((g * 2 + r) * LANES, (g * 2 + r + 1) * LANES)
            maps.append((dq0_ref, sl, dk_ref, gs, gs, MLA_HEADS + (g * 2 + r) * 2))
            maps.append((dq1_ref, sl, dk_ref, gs, gs, MLA_HEADS + (g * 2 + r) * 2 + 1))

    def scores(n, keep):
        q_ref, qs, k_ref, ks, _, _ = maps[n]
        st = _dot_nt(k_ref[:, ks], q_ref[:, qs])
        if keep is not None:
            st = jnp.where(keep, st, NEG)
        s_scr[n % ATTN_SLOTS] = st

    def softmax_pv(n):
        _, _, _, _, vs, idx = maps[n]
        vt = ckvt_ref[...] if vs is None else dvt_ref[vs, :]
        st = s_scr[n % ATTN_SLOTS]
        m_prev = m_s[idx]
        m_new = jnp.maximum(m_prev, jnp.max(_col_tree(st, jnp.maximum), axis=0, keepdims=True))
        alpha = jnp.exp2(m_prev - m_new)
        p = jnp.exp2(st - m_new)
        l_s[idx] = alpha * l_s[idx] + jnp.sum(_col_tree(p, jnp.add), axis=0, keepdims=True)
        acc_s[idx] = alpha * acc_s[idx] + _dot(vt, p.astype(BF16))
        m_s[idx] = m_new

    def step(masked):
        keep = None
        if masked:
            keep = lax.broadcasted_iota(jnp.int32, (tq, tq), 0) <= lax.broadcasted_iota(jnp.int32, (tq, tq), 1)
        scores(0, keep)
        for n in range(ATTN_MAPS):
            if n + 1 < ATTN_MAPS:
                scores(n + 1, keep)
            softmax_pv(n)

    @pl.when(j < i)
    def _():
        step(False)

    @pl.when(j == i)
    def _():
        step(True)
        outs = []
        for h in range(MLA_HEADS):
            lat_t = (acc_s[h] / l_s[h]).astype(BF16)
            outs.append(_dot(wuvt_ref[h], lat_t))
        lam = _lam(lq1_ref, lk1_ref, lq2_ref, lk2_ref, lam_init)
        for gr in range(DIFF_HEADS):
            i0 = MLA_HEADS + 2 * gr
            d = acc_s[i0] / l_s[i0] - lam * (acc_s[i0 + 1] / l_s[i0 + 1])
            d = d * lax.rsqrt(jnp.mean(d * d, axis=0, keepdims=True) + EPS) * gsub_ref[...]
            outs.append(d * (1.0 - lam_init))
        out_ref[...] = jnp.concatenate(outs, axis=0).T.astype(out_ref.dtype)


def _attn(qf, dq0, dq1, kf, ckvt, dkb, dvt, wuvt, gsub_col, lams, lam_init, tq):
    b, s, _ = qf.shape
    nq = s // tq
    assert s % tq == 0 and tq % 64 == 0

    def q_spec(w):
        return pl.BlockSpec((None, tq, w), lambda bb, i, j: (bb, i, 0))

    def k_spec(w):
        return pl.BlockSpec((None, tq, w), lambda bb, i, j: (bb, jnp.minimum(i, j), 0))

    def kt_spec(w):
        return pl.BlockSpec((None, w, tq), lambda bb, i, j: (bb, 0, jnp.minimum(i, j)))

    mix_w = MLA_HEADS * MLA_V + DIFF_HEADS * DIFF_VD
    return pl.pallas_call(
        functools.partial(_attn_kernel, tq=tq, lam_init=lam_init),
        grid=(b, nq, nq),
        in_specs=[q_spec(QK_W), q_spec(DQ_W), q_spec(DQ_W), k_spec(QK_W), kt_spec(MLA_KV_RANK), k_spec(DK_W),
                  kt_spec(DV_W), _const_spec(wuvt.shape), _const_spec(gsub_col.shape)]
                 + [_const_spec(l.shape) for l in lams],
        out_specs=q_spec(mix_w),
        out_shape=jax.ShapeDtypeStruct((b, s, mix_w), BF16),
        scratch_shapes=[pltpu.VMEM((ATTN_MAPS, 1, tq), F32), pltpu.VMEM((ATTN_MAPS, 1, tq), F32),
                        pltpu.VMEM((ATTN_MAPS, DIFF_VD, tq), F32), pltpu.VMEM((ATTN_SLOTS, tq, tq), F32)],
        compiler_params=_cparams(("parallel", "parallel", "arbitrary")),
        name="attn",
    )(qf, dq0, dq1, kf, ckvt, dkb, dvt, wuvt, gsub_col, *lams)


def _qabs_kernel(qf_ref, gk_ref, wt_ref, qa_ref):
    for h in range(MLA_HEADS):
        sl = slice(h * LANES, (h + 1) * LANES)
        qg = (qf_ref[:, sl].astype(F32) * gk_ref[...]).astype(BF16)
        qa_ref[:, sl] = _dot(qg, wt_ref[h])


def _qabs(qf, gk_row, wuk_t):
    n = qf.shape[0]
    return pl.pallas_call(
        _qabs_kernel,
        grid=(1,),
        in_specs=[_const_spec(qf.shape), _const_spec(gk_row.shape), _const_spec(wuk_t.shape)],
        out_specs=_const_spec((n, QK_W)),
        out_shape=jax.ShapeDtypeStruct((n, QK_W), F32),
        compiler_params=_cparams(("arbitrary",)),
        name="qabs",
    )(qf, gk_row, wuk_t)


def _dec_kernel(pt_ref, qa_ref, qp_ref, qbd_ref, qf8_ref, kf8_ref, dkrow_ref, ckvrow_ref, dvrow_ref,
                wukt_ref, wuv_ref, gsub_ref, lq1_ref, lk1_ref, lq2_ref, lk2_ref,
                ckv_hbm, kpe_hbm, kt_hbm, v_hbm, oa_ref, ob_ref,
                ckv_buf, kpe_buf, kt_buf, v_buf, sems, lhs_ref, m_a, l_a, acc_a, m_d, l_d, acc_d,
                *, layer, n_seq, n_pages, lam_init):
    pp = DEC_PAGES
    n_steps = n_pages // pp
    total = n_seq * n_steps
    hbm = (ckv_hbm, kpe_hbm, kt_hbm, v_hbm)
    bufs = (ckv_buf, kpe_buf, kt_buf, v_buf)
    nk = MLA_HEADS * MLA_NOPE
    row = lax.broadcasted_iota(jnp.int32, (MLA_HEADS, 1), 0)
    row_g0 = ((row >> 1) & 1) == 0

    def page_copy(a, slot, k, page):
        return pltpu.make_async_copy(hbm[a].at[layer, page], bufs[a].at[slot, k], sems.at[a, slot])

    def start_step(t, slot):
        b = t // n_steps
        first = (t - b * n_steps) * pp
        for k in range(pp):
            page = pt_ref[b, first + k]
            for a in range(len(hbm)):
                page_copy(a, slot, k, page).start()

    def wait_step(slot):
        for k in range(pp):
            for a in range(len(hbm)):
                page_copy(a, slot, k, 0).wait()

    def update(s, m_ref, l_ref, c):
        m_prev = m_ref[c]
        m_new = jnp.maximum(m_prev, jnp.max(s, axis=-1, keepdims=True))
        alpha = jnp.exp2(m_prev - m_new)
        p = jnp.exp2(s - m_new)
        l_ref[c] = alpha * l_ref[c] + jnp.sum(p, axis=-1, keepdims=True)
        m_ref[c] = m_new
        return alpha, p

    def pages(c):
        return range(c * DEC_CHAIN_PAGES, (c + 1) * DEC_CHAIN_PAGES)

    def body(t, carry):
        slot = t & 1
        b = t // n_steps
        j = t - b * n_steps
        start_step(jnp.minimum(t + 1, total - 1), 1 - slot)
        wait_step(slot)

        @pl.when(j == 0)
        def _():
            lhs_ref[:nk, :] = wukt_ref[...]
            lhs_ref[nk:, :] = jnp.concatenate(
                [qa_ref[b].astype(BF16), jnp.zeros((lhs_ref.shape[0] - nk - MLA_HEADS, LANES), BF16)], axis=0)
            for r in (m_a, m_d):
                r[...] = jnp.full(r.shape, NEG, F32)
            for r in (l_a, acc_a, l_d, acc_d):
                r[...] = jnp.zeros(r.shape, F32)

        qp = qp_ref[b]
        qbd = qbd_ref[b]

        def score_products(c):
            ckv = jnp.concatenate([ckv_buf[slot, k] for k in pages(c)], axis=0).astype(BF16)
            res = _dot_nt(lhs_ref[...], ckv)
            kpe_t = jnp.concatenate([kpe_buf[slot, k] for k in pages(c)], axis=1).astype(BF16)
            bp = _dot(qp, kpe_t)
            kt = jnp.concatenate([kt_buf[slot, k] for k in pages(c)], axis=1).astype(BF16)
            return ckv, res, bp, _dot(qbd, kt)

        def softmax(c, res, bp, sd):
            sq = res[:nk] * res[:nk]
            ssq = jnp.concatenate(
                [jnp.sum(sq[h * MLA_NOPE:(h + 1) * MLA_NOPE], axis=0, keepdims=True) for h in range(MLA_HEADS)],
                axis=0)
            rnorm = lax.rsqrt(ssq * (1.0 / MLA_NOPE) + EPS)
            alpha_a, p_a = update(res[nk:nk + MLA_HEADS] * rnorm + bp, m_a, l_a, c)
            alpha_d, p_d = update(sd, m_d, l_d, c)
            return alpha_a, p_a.astype(BF16), alpha_d, p_d.astype(BF16)

        def value_products(c, ckv, alpha_a, p_a, alpha_d, p_d):
            acc_a[c] = alpha_a * acc_a[c] + _dot(p_a, ckv)
            pv = []
            for g in range(DIFF_KV_HEADS):
                v = jnp.concatenate(
                    [v_buf[slot, k, pl.ds(g, PAGE_SIZE, stride=DIFF_KV_HEADS), :] for k in pages(c)], axis=0)
                pv.append(_dot(p_d, v.astype(BF16)))
            acc_d[c] = alpha_d * acc_d[c] + jnp.where(row_g0, pv[0], pv[1])

        prods = {}
        for step in range(DEC_CHAINS + DEC_SKEW):
            if step < DEC_CHAINS:
                prods[step] = score_products(step)
            c = step - DEC_SKEW
            if c >= 0:
                ckv, res, bp, sd = prods.pop(c)
                value_products(c, ckv, *softmax(c, res, bp, sd))

        @pl.when(j == n_steps - 1)
        def _():
            def merged(s_self, v_self, m_ref, l_ref, acc_ref):
                m = s_self
                for c in range(DEC_CHAINS):
                    m = jnp.maximum(m, m_ref[c])
                p_self = jnp.exp2(s_self - m)
                l = p_self
                acc = p_self * v_self
                for c in range(DEC_CHAINS):
                    w = jnp.exp2(m_ref[c] - m)
                    l = l + w * l_ref[c]
                    acc = acc + w * acc_ref[c]
                return acc / l

            s_self = jnp.sum(qf8_ref[b].astype(F32) * kf8_ref[b].astype(F32), axis=-1, keepdims=True)
            lat = merged(s_self, ckvrow_ref[b], m_a, l_a, acc_a)
            full = _dot(lat.astype(BF16), wuv_ref[...])
            col_head = lax.broadcasted_iota(jnp.int32, full.shape, 1) >> 6
            row_head = lax.broadcasted_iota(jnp.int32, full.shape, 0)
            oa_ref[b] = jnp.sum(jnp.where(col_head == row_head, full, 0.0), axis=0, keepdims=True)
            s_self = jnp.sum(qbd.astype(F32) * dkrow_ref[b].astype(F32), axis=-1, keepdims=True)
            dvrow = dvrow_ref[b]
            v_self = jnp.where(row_g0, dvrow[:, :DIFF_VD], dvrow[:, DIFF_VD:])
            o = merged(s_self, v_self, m_d, l_d, acc_d)
            lam = _lam(lq1_ref, lk1_ref, lq2_ref, lk2_ref, lam_init)
            d = o[:DIFF_HEADS] - lam * o[DIFF_HEADS:]
            ob_ref[b] = _rms(d, gsub_ref[...]) * (1.0 - lam_init)

        return carry

    start_step(0, 0)
    lax.fori_loop(0, total, body, 0)
    wait_step(total & 1)


def _decode(page_table, per_seq, consts, caches, layer, lam_init):
    b, n_pages = page_table.shape
    pp = DEC_PAGES

    def full_spec(a):
        return pl.BlockSpec(a.shape, lambda i, pt: (0,) * a.ndim)

    oa_w = MLA_HEADS * MLA_V
    out_shape = [jax.ShapeDtypeStruct((b, 1, oa_w), F32), jax.ShapeDtypeStruct((b, DIFF_HEADS, DIFF_VD), F32)]
    grid_spec = pltpu.PrefetchScalarGridSpec(
        num_scalar_prefetch=1,
        grid=(1,),
        in_specs=[full_spec(a) for a in list(per_seq) + list(consts)]
                 + [pl.BlockSpec(memory_space=pl.ANY)] * len(caches),
        out_specs=[full_spec(o) for o in out_shape],
        scratch_shapes=[pltpu.VMEM((2, pp) + c.shape[2:], c.dtype) for c in caches]
                       + [pltpu.SemaphoreType.DMA((len(caches), 2)),
                          pltpu.VMEM((MLA_HEADS * MLA_NOPE + 16, LANES), BF16),
                          pltpu.VMEM((DEC_CHAINS, MLA_HEADS, 1), F32), pltpu.VMEM((DEC_CHAINS, MLA_HEADS, 1), F32),
                          pltpu.VMEM((DEC_CHAINS, MLA_HEADS, MLA_KV_RANK), F32),
                          pltpu.VMEM((DEC_CHAINS, MLA_HEADS, 1), F32), pltpu.VMEM((DEC_CHAINS, MLA_HEADS, 1), F32),
                          pltpu.VMEM((DEC_CHAINS, MLA_HEADS, DIFF_VD), F32)],
    )
    return pl.pallas_call(
        functools.partial(_dec_kernel, layer=layer, n_seq=b, n_pages=n_pages, lam_init=lam_init),
        grid_spec=grid_spec,
        out_shape=out_shape,
        compiler_params=_cparams(("arbitrary",)),
        name="decode",
    )(page_table, *per_seq, *consts, *caches)


def _out_kernel(mix_ref, x_ref, gt_ref, sc_ref, sh_ref, wo_ref, g2_ref, wrh_ref, wrl_ref, br_ref,
                x1_ref, h2_ref, gate_ref):
    o = _dot(mix_ref[...], wo_ref[...])
    x1 = x_ref[...] + gt_ref[...] * o
    x1_ref[...] = x1
    h2 = _rms(x1, g2_ref[...]) * (1.0 + sc_ref[...]) + sh_ref[...]
    h2_ref[...] = h2.astype(BF16)
    hh, hl = _split(h2)
    logits = _dot(hh, wrh_ref[...]) + _dot(hh, wrl_ref[...]) + _dot(hl, wrh_ref[...]) + br_ref[...]
    lane_i = lax.broadcasted_iota(jnp.int32, logits.shape, 1)
    lane = lane_i.astype(F32)
    big = float(ROUTER_LANES)
    gl = jnp.where(lane_i < N_GROUPS, logits, NEG)
    gmax = jnp.max(gl, axis=-1, keepdims=True)
    gidx = jnp.min(jnp.where(gl == gmax, lane, big), axis=-1, keepdims=True)
    g_w = 1.0 / jnp.sum(jnp.exp(gl - gmax), axis=-1, keepdims=True)
    in_group = (lane_i >= N_GROUPS) & (lane_i < N_GROUPS + N_ROUTED) & (
        ((lane_i - N_GROUPS) >> 3).astype(F32) == gidx)
    el = jnp.where(in_group, logits, NEG)
    e1 = jnp.max(el, axis=-1, keepdims=True)
    i1 = jnp.min(jnp.where(el == e1, lane, big), axis=-1, keepdims=True)
    el2 = jnp.where(lane == i1, NEG, el)
    e2 = jnp.max(el2, axis=-1, keepdims=True)
    i2 = jnp.min(jnp.where(el2 == e2, lane, big), axis=-1, keepdims=True)
    t = jnp.exp(e2 - e1)
    w1 = 1.0 / (1.0 + t)
    w2 = t / (1.0 + t)
    gate_ref[...] = jnp.where(lane == i1, w1, jnp.where(lane == i2, w2, 0.0)) * g_w


def _out(mix, x3, gt, sc, sh, wo, g2, wrh, wrl, br, tm):
    b, t, d = x3.shape
    per_tok = gt.shape[1] != 1

    def tok_spec(w):
        return pl.BlockSpec((None, tm, w), lambda s, bb: (bb, s, 0))

    mod_spec = tok_spec(d) if per_tok else pl.BlockSpec((None, 1, d), lambda s, bb: (bb, 0, 0))
    consts = [wo, g2, wrh, wrl, br]
    return pl.pallas_call(
        _out_kernel,
        grid=(t // tm, b),
        in_specs=[tok_spec(mix.shape[2]), tok_spec(d), mod_spec, mod_spec, mod_spec] + [_const_spec(c.shape) for c in consts],
        out_specs=[tok_spec(d), tok_spec(d), tok_spec(ROUTER_LANES)],
        out_shape=[jax.ShapeDtypeStruct((b, t, d), F32), jax.ShapeDtypeStruct((b, t, d), BF16),
                   jax.ShapeDtypeStruct((b, t, ROUTER_LANES), F32)],
        compiler_params=_cparams(("parallel", "parallel")),
        name="out",
    )(mix, x3, gt, sc, sh, *consts)


def _moe_kernel(h2_ref, gate_ref, x1_ref, gt_ref, wg_ref, wu_ref, wd_ref, ex_ref, y_ref, acc_ref):
    g = pl.program_id(2)

    @pl.when(g == 0)
    def _():
        acc_ref[...] = jnp.zeros(acc_ref.shape, F32)

    h = h2_ref[...]
    a = _silu(_dot(h, wg_ref[...])) * _dot(h, wu_ref[...])
    ge = _dot(gate_ref[...].astype(BF16), ex_ref[...])
    acc_ref[...] += _dot((a * ge).astype(BF16), wd_ref[...])

    @pl.when(g == N_GROUPS - 1)
    def _():
        y_ref[...] = x1_ref[...] + gt_ref[...] * acc_ref[...]


def _moe(h2, gate, x1, gt, wg, wu, wd, ex, tm):
    b, t, d = x1.shape
    per_tok = gt.shape[1] != 1

    def tok_spec(w):
        return pl.BlockSpec((None, tm, w), lambda s, bb, g: (bb, s, 0))

    mod_spec = tok_spec(d) if per_tok else pl.BlockSpec((None, 1, d), lambda s, bb, g: (bb, 0, 0))

    def grp_spec(a):
        return pl.BlockSpec((None,) + a.shape[1:], lambda s, bb, g: (g, 0, 0))

    return pl.pallas_call(
        _moe_kernel,
        grid=(t // tm, b, N_GROUPS),
        in_specs=[tok_spec(d), tok_spec(ROUTER_LANES), tok_spec(d), mod_spec,
                  grp_spec(wg), grp_spec(wu), grp_spec(wd), grp_spec(ex)],
        out_specs=tok_spec(d),
        out_shape=jax.ShapeDtypeStruct((b, t, d), F32),
        scratch_shapes=[pltpu.VMEM((tm, d), F32)],
        compiler_params=_cparams(("parallel", "parallel", "arbitrary")),
        name="moe",
    )(h2, gate, x1, gt, wg, wu, wd, ex)


def _rope_tables(pos):
    def cs(dim):
        half = dim // 2
        inv = ROPE_THETA ** (-jnp.arange(half, dtype=F32) * 2.0 / dim)
        ang = pos[:, None] * inv[None, :]
        c, s = jnp.cos(ang), jnp.sin(ang)
        return jnp.concatenate([c, c], axis=1), jnp.concatenate([-s, s], axis=1)

    t = pos.shape[0]
    c32, s32 = cs(MLA_ROPE)
    c64, s64 = cs(DIFF_HD)
    z = lambda w: jnp.zeros((t, w), F32)
    pad = LANES - MLA_NOPE - MLA_ROPE
    cosq = jnp.concatenate([jnp.ones((t, MLA_NOPE), F32), c32, z(pad)], axis=1)
    sinq = jnp.concatenate([z(MLA_NOPE), s32, z(pad)], axis=1)
    cosk = jnp.concatenate([c32, z(LANES - MLA_ROPE)], axis=1)
    sink = jnp.concatenate([s32, z(LANES - MLA_ROPE)], axis=1)
    cosd = jnp.concatenate([c64, c64], axis=1)
    sind = jnp.concatenate([s64, s64], axis=1)
    return [cosq, sinq, cosk, sink, cosd, sind]


def _block_diag_mean(sizes, width):
    m = jnp.zeros((width, width), F32)
    o = 0
    while o < width:
        for sz in sizes:
            if sz > 0:
                m = m.at[o:o + sz, o:o + sz].set(1.0 / sz)
            o += abs(sz)
    return m.astype(BF16)


def _layer_weights(l, w_in, g_norm1, g_mla_qa, w_mla_uq, g_mla_kva, w_mla_uk, g_mla_qn_nope, g_mla_qn_rope,
                   g_mla_kn_nope, g_mla_kn_rope, g_diff_qn, g_diff_kn):
    d = w_in.shape[1]
    o_kpe = MLA_Q_RANK + MLA_KV_RANK
    wi = w_in[l]
    win = jnp.concatenate([wi[:, :o_kpe], wi[:, o_kpe:o_kpe + MLA_ROPE], jnp.zeros((d, LANES - MLA_ROPE), F32),
                           wi[:, o_kpe + MLA_ROPE:]], axis=1).astype(BF16)
    pad = LANES - MLA_NOPE - MLA_ROPE
    wuq = w_mla_uq[l].reshape(MLA_Q_RANK, MLA_HEADS, MLA_NOPE + MLA_ROPE)
    wuq = jnp.concatenate([wuq, jnp.zeros((MLA_Q_RANK, MLA_HEADS, pad), F32)], axis=2).reshape(MLA_Q_RANK, QK_W).astype(BF16)
    wuk = jnp.concatenate([w_mla_uk[l], jnp.zeros((MLA_KV_RANK, MLA_HEADS, LANES - MLA_NOPE), F32)], axis=2)
    wuk = wuk.reshape(MLA_KV_RANK, QK_W).astype(BF16)
    gq = jnp.tile(jnp.concatenate([g_mla_qn_nope[l], g_mla_qn_rope[l], jnp.zeros((pad,), F32)]), MLA_HEADS)[None]
    gk = jnp.tile(jnp.concatenate([g_mla_kn_nope[l], jnp.zeros((LANES - MLA_NOPE,), F32)]), MLA_HEADS)[None]
    gkpe = jnp.concatenate([g_mla_kn_rope[l], jnp.zeros((LANES - MLA_ROPE,), F32)])[None]
    gdq = jnp.tile(g_diff_qn[l], DQ_W // DIFF_HD)[None]
    gdk = jnp.tile(g_diff_kn[l], DK_W // DIFF_HD)[None]
    bdq = _block_diag_mean((MLA_NOPE, MLA_ROPE, -pad), MXU_DIM)
    bdd = _block_diag_mean((DIFF_HD,), MXU_DIM)
    return [g_norm1[l][None], win, g_mla_qa[l][None], wuq, gq, g_mla_kva[l][None], wuk, gk, gkpe, gdq, gdk, bdq, bdd]


def kernel(x_prompt, x_sample, cache_mla_ckv, cache_mla_kpe, cache_diff_k, cache_diff_v, page_table, c_prompt, c_sample, w_ada, b_ada, g_norm1, w_in, g_mla_qa, w_mla_uq, g_mla_kva, w_mla_uk, w_mla_uv, g_mla_qn_nope, g_mla_qn_rope, g_mla_kn_nope, g_mla_kn_rope, g_diff_qn, g_diff_kn, lam_q1, lam_k1, lam_q2, lam_k2, g_diff_subln, w_o, g_norm2, w_router_group, b_router_group, w_router_expert, b_router_expert, w_exp_gate, w_exp_up, w_exp_down):
    bp, sp, d = x_prompt.shape
    bs, ts, _ = x_sample.shape
    depth = w_in.shape[0]
    n_pool = cache_mla_ckv.shape[1]
    n_pages = page_table.shape[1]
    assert ts == 1 and n_pages % DEC_PAGES == 0 and cache_mla_ckv.shape[2] == PAGE_SIZE
    assert w_in.shape[2] == PROJ_W - LANES + MLA_ROPE and d % MXU_DIM == 0
    past = n_pages * PAGE_SIZE

    tm_p = min(256, sp)
    tq = min(512, sp)
    tm_o = min(512, sp)
    tables_p = _rope_tables(jnp.arange(sp, dtype=F32))
    tables_s = _rope_tables(jnp.arange(ts, dtype=F32) + past)

    kpe_t = jnp.transpose(cache_mla_kpe, (0, 1, 3, 2))
    k_t = jnp.transpose(cache_diff_k, (0, 1, 3, 4, 5, 2)).reshape(depth, n_pool, DK_W, PAGE_SIZE)
    v_rows = cache_diff_v.reshape(depth, n_pool, PAGE_SIZE * DIFF_KV_HEADS, DIFF_VD)

    xp = x_prompt
    xs = x_sample.reshape(1, bs, d)
    outs_p = [[], [], [], []]
    outs_s = [[], [], [], []]
    for l in range(depth):
        lam_init = 0.8 - 0.6 * math.exp(-0.3 * l)
        wts = _layer_weights(l, w_in, g_norm1, g_mla_qa, w_mla_uq, g_mla_kva, w_mla_uk, g_mla_qn_nope,
                             g_mla_qn_rope, g_mla_kn_nope, g_mla_kn_rope, g_diff_qn, g_diff_kn)
        lams = [lam_q1[l][None], lam_k1[l][None], lam_q2[l][None], lam_k2[l][None]]
        gsub = g_diff_subln[l][None]
        wuv = w_mla_uv[l]
        wuv_flat = wuv.reshape(MLA_KV_RANK, MLA_HEADS * MLA_V).astype(BF16)
        wuvt = jnp.transpose(wuv, (1, 2, 0)).astype(BF16)
        wuk_t = jnp.concatenate([jnp.transpose(w_mla_uk[l], (1, 2, 0)),
                                 jnp.zeros((MLA_HEADS, LANES - MLA_NOPE, MLA_KV_RANK), F32)], axis=1).astype(BF16)
        wukt_rows = jnp.transpose(w_mla_uk[l], (1, 2, 0)).reshape(MLA_HEADS * MLA_NOPE, MLA_KV_RANK).astype(BF16)
        gk_row = jnp.concatenate([g_mla_kn_nope[l], jnp.zeros((LANES - MLA_NOPE,), F32)])[None]
        wo = w_o[l].astype(BF16)
        wr = jnp.concatenate([w_router_group[l], jnp.transpose(w_router_expert[l], (1, 0, 2)).reshape(d, N_ROUTED),
                              jnp.zeros((d, ROUTER_LANES - N_GROUPS - N_ROUTED), F32)], axis=1)
        wrh, wrl = _split(wr)
        br = jnp.concatenate([b_router_group[l], b_router_expert[l].reshape(N_ROUTED),
                              jnp.zeros((ROUTER_LANES - N_GROUPS - N_ROUTED,), F32)])[None]
        wg = jnp.transpose(w_exp_gate[l], (0, 2, 1, 3)).reshape(N_GROUPS, d, GROUP_FF).astype(BF16)
        wu = jnp.transpose(w_exp_up[l], (0, 2, 1, 3)).reshape(N_GROUPS, d, GROUP_FF).astype(BF16)
        wd = w_exp_down[l].reshape(N_GROUPS, GROUP_FF, d).astype(BF16)
        lane = jnp.arange(ROUTER_LANES)[None, :, None]
        col = jnp.arange(GROUP_FF)[None, None, :]
        grp = jnp.arange(N_GROUPS)[:, None, None]
        ex = (lane == N_GROUPS + grp * EXPERTS_PER_GROUP + col // EXPERT_FF).astype(BF16)

        mod = _ada(jnp.concatenate([c_prompt, c_sample], axis=0), w_ada[l], b_ada[l][None])
        mod_p = mod[:bp].reshape(bp, 6, 1, d)
        mod_s = mod[bp:].reshape(1, bs, 6, d)
        sh1p, sc1p, gt1p, sh2p, sc2p, gt2p = [mod_p[:, k] for k in range(6)]
        sh1s, sc1s, gt1s, sh2s, sc2s, gt2s = [mod_s[:, :, k] for k in range(6)]

        (qf, kf, dq0, dq1, ckvt, dkb, dvt, ckv, kpe, dk, dv) = _proj(xp, sc1p, sh1p, tables_p, wts, tm_p)
        mix = _attn(qf, dq0, dq1, kf, ckvt, dkb, dvt, wuvt, gsub.reshape(DIFF_VD, 1), lams, lam_init, tq)
        x1, h2, gate = _out(mix, xp, gt1p, sc2p, sh2p, wo, g_norm2[l][None], wrh, wrl, br, tm_o)
        xp = _moe(h2, gate, x1, gt2p, wg, wu, wd, ex, tm_o)
        for lst, a in zip(outs_p, (ckv, kpe, dk, dv)):
            lst.append(a)

        (qf, kf, dq0, dq1, _, dkb, _, ckv, kpe, dk, dv) = _proj(xs, sc1s, sh1s, tables_s, wts, bs)
        qf2 = qf.reshape(bs, QK_W)
        qa = _qabs(qf2, gk_row, wuk_t).reshape(bs, MLA_HEADS, LANES)
        qf8 = qf2.reshape(bs, MLA_HEADS, LANES)
        kf8 = kf.reshape(bs, MLA_HEADS, LANES)
        qp = qf8[:, :, MLA_NOPE:MLA_NOPE + MLA_ROPE]
        dq5 = (dq0 + dq1).reshape(bs, DIFF_KV_HEADS, 2, 2, DIFF_HD)
        eye = jnp.eye(2, dtype=BF16)
        qbd = jnp.einsum('bgrmd,gh,mn->bmgrhnd', dq5, eye, eye).reshape(bs, 2 * DIFF_HEADS, DK_W)
        per_seq = [qa, qp, qbd, qf8, kf8, dkb.reshape(bs, 1, DK_W), ckv.reshape(bs, 1, MLA_KV_RANK),
                   dv.reshape(bs, 1, DV_W)]
        consts = [wukt_rows, wuv_flat, gsub] + lams
        oa, ob = _decode(page_table, per_seq, consts, (cache_mla_ckv, kpe_t, k_t, v_rows), l, lam_init)
        mix = jnp.concatenate([oa.reshape(bs, -1), ob.reshape(bs, -1)], axis=1).astype(BF16).reshape(1, bs, -1)
        x1, h2, gate = _out(mix, xs, gt1s, sc2s, sh2s, wo, g_norm2[l][None], wrh, wrl, br, bs)
        xs = _moe(h2, gate, x1, gt2s, wg, wu, wd, ex, bs)
        for lst, a in zip(outs_s, (ckv, kpe, dk, dv)):
            lst.append(a)

    def stack_p(lst, tail):
        return jnp.stack(lst).reshape((depth, bp, sp) + tail)

    def stack_s(lst, tail):
        return jnp.stack(lst).reshape((depth, bs, ts) + tail)

    k_tail = (DIFF_KV_HEADS, 2, DIFF_HD)
    v_tail = (DIFF_KV_HEADS, DIFF_VD)
    return (xp, xs.reshape(bs, ts, d),
            stack_p(outs_p[0], (MLA_KV_RANK,)), stack_p(outs_p[1], (MLA_ROPE,)), stack_p(outs_p[2], k_tail),
            stack_p(outs_p[3], v_tail),
            stack_s(outs_s[0], (MLA_KV_RANK,)), stack_s(outs_s[1], (MLA_ROPE,)), stack_s(outs_s[2], k_tail),
            stack_s(outs_s[3], v_tail))
```

```python
import functools
import math

import jax
import jax.numpy as jnp
from jax import lax
from jax.experimental import pallas as pl
from jax.experimental.pallas import tpu as pltpu

F32 = jnp.float32
BF16 = jnp.bfloat16

MLA_HEADS = 8
MLA_Q_RANK = 256
MLA_KV_RANK = 128
MLA_NOPE = 64
MLA_ROPE = 32
MLA_V = 64
DIFF_HEADS = 4
DIFF_KV_HEADS = 2
DIFF_HD = 64
DIFF_VD = 128
N_GROUPS = 4
EXPERTS_PER_GROUP = 8
EXPERT_FF = 128
PAGE_SIZE = 128
ROPE_THETA = 10000.0
EPS = 1e-6
LOG2E = 1.4426950408889634
MLA_SCALE = (MLA_NOPE + MLA_ROPE) ** -0.5
DIFF_SCALE = DIFF_HD ** -0.5
NEG = -1e30

LANES = 128
MXU_DIM = 256
VMEM_LIMIT = 56 * 1024 * 1024
ROUTER_LANES = 128
N_ROUTED = N_GROUPS * EXPERTS_PER_GROUP
GROUP_FF = EXPERTS_PER_GROUP * EXPERT_FF
QK_W = MLA_HEADS * LANES
DQ_W = DIFF_HEADS * 2 * DIFF_HD
DK_W = DIFF_KV_HEADS * 2 * DIFF_HD
DV_W = DIFF_KV_HEADS * DIFF_VD
PROJ_W = MLA_Q_RANK + MLA_KV_RANK + LANES + DQ_W + DK_W + DV_W
DEC_PAGES = 16
DEC_CHAIN_PAGES = 2
DEC_CHAINS = DEC_PAGES // DEC_CHAIN_PAGES
DEC_SKEW = 2


def _dot(a, b):
    return jnp.dot(a, b, preferred_element_type=F32)


def _dot_nt(a, b):
    return lax.dot_general(a, b, (((1,), (1,)), ((), ())), preferred_element_type=F32)


def _split(a):
    hi = a.astype(BF16)
    lo = (a - hi.astype(F32)).astype(BF16)
    return hi, lo


def _rms(v, g):
    return v * lax.rsqrt(jnp.mean(v * v, axis=-1, keepdims=True) + EPS) * g


def _silu(v):
    return v / (1.0 + jnp.exp(-v))


def _cparams(sem):
    return pltpu.CompilerParams(dimension_semantics=sem, vmem_limit_bytes=VMEM_LIMIT)


def _const_spec(shape):
    nd = len(shape)
    return pl.BlockSpec(shape, lambda *_: (0,) * nd)


def _ada_kernel(c_ref, w_ref, b_ref, o_ref):
    s = _silu(c_ref[...])
    sh, sl = _split(s)
    wh, wl = _split(w_ref[...])
    o_ref[...] = _dot(sh, wh) + _dot(sh, wl) + _dot(sl, wh) + b_ref[...]


def _ada(c, w, b):
    m, d = c.shape
    n = w.shape[1]
    tn = 512
    return pl.pallas_call(
        _ada_kernel,
        grid=(n // tn,),
        in_specs=[_const_spec((m, d)), pl.BlockSpec((d, tn), lambda i: (0, i)), pl.BlockSpec((1, tn), lambda i: (0, i))],
        out_specs=pl.BlockSpec((m, tn), lambda i: (0, i)),
        out_shape=jax.ShapeDtypeStruct((m, n), F32),
        compiler_params=_cparams(("parallel",)),
        name="ada",
    )(c, w, b)


def _block_norm(v, bd, g):
    w = v.shape[1]
    sq = (v * v).astype(BF16)
    ms = jnp.concatenate([_dot(sq[:, i:i + MXU_DIM], bd) for i in range(0, w, MXU_DIM)], axis=1)
    return v * lax.rsqrt(ms + EPS) * g


def _rope(v, cos, sin, half, first):
    parts = []
    for i in range(0, v.shape[1], LANES):
        s = v[:, i:i + LANES]
        rot = jnp.where(first, pltpu.roll(s, LANES - half, 1), pltpu.roll(s, half, 1))
        parts.append(s * cos + rot * sin)
    return parts[0] if len(parts) == 1 else jnp.concatenate(parts, axis=1)


def _proj_kernel(x_ref, sc_ref, sh_ref, cosq_ref, sinq_ref, cosk_ref, sink_ref, cosd_ref, sind_ref,
                 g1_ref, win_ref, gqa_ref, wuq_ref, gq_ref, gkva_ref, wuk_ref, gk_ref, gkpe_ref, gdq_ref, gdk_ref,
                 bdq_ref, bdd_ref,
                 qf_ref, kf_ref, dq0_ref, dq1_ref, ckvt_ref, dkb_ref, dvt_ref, ckv_ref, kpe_ref, dk_ref, dv_ref):
    lane = lax.broadcasted_iota(jnp.int32, (1, LANES), 1)
    x = x_ref[...]
    h = _rms(x, g1_ref[...]) * (1.0 + sc_ref[...]) + sh_ref[...]
    proj = _dot(h.astype(BF16), win_ref[...])
    o_ckv = MLA_Q_RANK
    o_kpe = o_ckv + MLA_KV_RANK
    o_dq = o_kpe + LANES
    o_dk = o_dq + DQ_W
    o_dv = o_dk + DK_W

    cqn = _rms(proj[:, :MLA_Q_RANK], gqa_ref[...])
    q = _dot(cqn.astype(BF16), wuq_ref[...])
    qn = _block_norm(q, bdq_ref[...], gq_ref[...])
    q_first = (lane >= MLA_NOPE) & (lane < MLA_NOPE + MLA_ROPE // 2)
    qf = _rope(qn, cosq_ref[...], sinq_ref[...], MLA_ROPE // 2, q_first)
    qf_ref[...] = (qf * (MLA_SCALE * LOG2E)).astype(BF16)

    ckv = _rms(proj[:, o_ckv:o_kpe], gkva_ref[...])
    ckv_ref[...] = ckv
    ckvb = ckv.astype(BF16)
    ckvt_ref[...] = ckv.T.astype(BF16)
    kr = proj[:, o_kpe:o_dq]
    kn = kr * lax.rsqrt(jnp.sum(kr * kr, axis=-1, keepdims=True) * (1.0 / MLA_ROPE) + EPS) * gkpe_ref[...]
    kpe = _rope(kn, cosk_ref[...], sink_ref[...], MLA_ROPE // 2, lane < MLA_ROPE // 2)
    kpe_ref[...] = kpe[:, :MLA_ROPE]

    kraw = _dot(ckvb, wuk_ref[...])
    knn = _block_norm(kraw, bdq_ref[...], gk_ref[...])
    kpe_at_rope = pltpu.roll(kpe, MLA_NOPE, 1)
    kf_ref[...] = jnp.concatenate(
        [knn[:, i:i + LANES] + kpe_at_rope for i in range(0, QK_W, LANES)], axis=1).astype(BF16)

    d_first = (lane & (DIFF_HD - 1)) < DIFF_HD // 2
    dq = _block_norm(proj[:, o_dq:o_dk], bdd_ref[...], gdq_ref[...])
    dq = _rope(dq, cosd_ref[...], sind_ref[...], DIFF_HD // 2, d_first) * (DIFF_SCALE * LOG2E)
    map0 = (lax.broadcasted_iota(jnp.int32, (1, DQ_W), 1) & (LANES - 1)) < DIFF_HD
    dq0_ref[...] = jnp.where(map0, dq, 0.0).astype(BF16)
    dq1_ref[...] = jnp.where(map0, 0.0, dq).astype(BF16)
    dk = _block_norm(proj[:, o_dk:o_dv], bdd_ref[...], gdk_ref[...])
    dk = _rope(dk, cosd_ref[...], sind_ref[...], DIFF_HD // 2, d_first)
    dk_ref[...] = dk
    dkb_ref[...] = dk.astype(BF16)
    dv = proj[:, o_dv:]
    dv_ref[...] = dv
    dvt_ref[...] = dv.T.astype(BF16)


def _proj(x3, sc, sh, tables, wts, tm):
    b, t, d = x3.shape
    nt = t // tm
    per_tok = sc.shape[1] != 1
    tab_rows = tables[0].shape[0]

    def tok_spec(w):
        return pl.BlockSpec((None, tm, w), lambda s, bb: (bb, s, 0))

    mod_spec = tok_spec(d) if per_tok else pl.BlockSpec((None, 1, d), lambda s, bb: (bb, 0, 0))
    tab_spec = (pl.BlockSpec((tm, LANES), lambda s, bb: (s, 0)) if tab_rows != 1
                else pl.BlockSpec((1, LANES), lambda s, bb: (0, 0)))
    in_specs = [tok_spec(d), mod_spec, mod_spec] + [tab_spec] * 6 + [_const_spec(w.shape) for w in wts]
    outs = [(QK_W, BF16, False), (QK_W, BF16, False), (DQ_W, BF16, False), (DQ_W, BF16, False),
            (MLA_KV_RANK, BF16, True), (DK_W, BF16, False), (DV_W, BF16, True),
            (MLA_KV_RANK, F32, False), (MLA_ROPE, F32, False), (DK_W, F32, False), (DV_W, F32, False)]

    def out_spec(w, tr):
        return pl.BlockSpec((None, w, tm), lambda s, bb: (bb, 0, s)) if tr else tok_spec(w)

    return pl.pallas_call(
        _proj_kernel,
        grid=(nt, b),
        in_specs=in_specs,
        out_specs=[out_spec(w, tr) for w, _, tr in outs],
        out_shape=[jax.ShapeDtypeStruct((b, w, t) if tr else (b, t, w), dt) for w, dt, tr in outs],
        compiler_params=_cparams(("parallel", "parallel")),
        name="proj",
    )(x3, sc, sh, *tables, *wts)


def _lam(lq1_ref, lk1_ref, lq2_ref, lk2_ref, lam_init):
    a = jnp.sum(lq1_ref[...] * lk1_ref[...], axis=-1, keepdims=True)
    b = jnp.sum(lq2_ref[...] * lk2_ref[...], axis=-1, keepdims=True)
    return jnp.exp(a) - jnp.exp(b) + lam_init


ATTN_MAPS = MLA_HEADS + 2 * DIFF_HEADS
ATTN_SLOTS = 2


def _col_tree(x, op):
    parts = [x[c * 64:(c + 1) * 64] for c in range(x.shape[0] // 64)]
    while len(parts) > 1:
        parts = [op(parts[a], parts[a + 1]) for a in range(0, len(parts), 2)]
    return parts[0]


def _attn_kernel(qf_ref, dq0_ref, dq1_ref, kf_ref, ckvt_ref, dk_ref, dvt_ref, wuvt_ref, gsub_ref,
                 lq1_ref, lk1_ref, lq2_ref, lk2_ref, out_ref, m_s, l_s, acc_s, s_scr, *, tq, lam_init):
    i = pl.program_id(1)
    j = pl.program_id(2)

    @pl.when(j == 0)
    def _():
        m_s[...] = jnp.full(m_s.shape, NEG, F32)
        l_s[...] = jnp.zeros(l_s.shape, F32)
        acc_s[...] = jnp.zeros(acc_s.shape, F32)

    maps = []
    for h in range(MLA_HEADS):
        sl = slice(h * LANES, (h + 1) * LANES)
        maps.append((qf_ref, sl, kf_ref, sl, None, h))
    for g in range(DIFF_KV_HEADS):
        gs = slice(g * LANES, (g + 1) * LANES)
        for r in range(DIFF_HEADS // DIFF_KV_HEADS):
            sl = slice((g * 2 + r) * LANES, (g * 2 + r + 1) * LANES)
            maps.append((dq0_ref, sl, dk_ref, gs, gs, MLA_HEADS + (g * 2 + r) * 2))
            maps.append((dq1_ref, sl, dk_ref, gs, gs, MLA_HEADS + (g * 2 + r) * 2 + 1))

    def scores(n, keep):
        q_ref, qs, k_ref, ks, _, _ = maps[n]
        st = _dot_nt(k_ref[:, ks], q_ref[:, qs])
        if keep is not None:
            st = jnp.where(keep, st, NEG)
        s_scr[n % ATTN_SLOTS] = st

    def softmax_pv(n):
        _, _, _, _, vs, idx = maps[n]
        vt = ckvt_ref[...] if vs is None else dvt_ref[vs, :]
        st = s_scr[n % ATTN_SLOTS]
        m_prev = m_s[idx]
        m_new = jnp.maximum(m_prev, jnp.max(_col_tree(st, jnp.maximum), axis=0, keepdims=True))
        alpha = jnp.exp2(m_prev - m_new)
        p = jnp.exp2(st - m_new)
        l_s[idx] = alpha * l_s[idx] + jnp.sum(_col_tree(p, jnp.add), axis=0, keepdims=True)
        acc_s[idx] = alpha * acc_s[idx] + _dot(vt, p.astype(BF16))
        m_s[idx] = m_new

    def step(masked):
        keep = None
        if masked:
            keep = lax.broadcasted_iota(jnp.int32, (tq, tq), 0) <= lax.broadcasted_iota(jnp.int32, (tq, tq), 1)
        scores(0, keep)
        for n in range(ATTN_MAPS):
            if n + 1 < ATTN_MAPS:
                scores(n + 1, keep)
            softmax_pv(n)

    @pl.when(j < i)
    def _():
        step(False)

    @pl.when(j == i)
    def _():
        step(True)
        outs = []
        for h in range(MLA_HEADS):
            lat_t = (acc_s[h] / l_s[h]).astype(BF16)
            outs.append(_dot(wuvt_ref[h], lat_t))
        lam = _lam(lq1_ref, lk1_ref, lq2_ref, lk2_ref, lam_init)
        for gr in range(DIFF_HEADS):
            i0 = MLA_HEADS + 2 * gr
            d = acc_s[i0] / l_s[i0] - lam * (acc_s[i0 + 1] / l_s[i0 + 1])
            d = d * lax.rsqrt(jnp.mean(d * d, axis=0, keepdims=True) + EPS) * gsub_ref[...]
            outs.append(d * (1.0 - lam_init))
        out_ref[...] = jnp.concatenate(outs, axis=0).T.astype(out_ref.dtype)


def _attn(qf, dq0, dq1, kf, ckvt, dkb, dvt, wuvt, gsub_col, lams, lam_init, tq):
    b, s, _ = qf.shape
    nq = s // tq
    assert s % tq == 0 and tq % 64 == 0

    def q_spec(w):
        return pl.BlockSpec((None, tq, w), lambda bb, i, j: (bb, i, 0))

    def k_spec(w):
        return pl.BlockSpec((None, tq, w), lambda bb, i, j: (bb, jnp.minimum(i, j), 0))

    def kt_spec(w):
        return pl.BlockSpec((None, w, tq), lambda bb, i, j: (bb, 0, jnp.minimum(i, j)))

    mix_w = MLA_HEADS * MLA_V + DIFF_HEADS * DIFF_VD
    return pl.pallas_call(
        functools.partial(_attn_kernel, tq=tq, lam_init=lam_init),
        grid=(b, nq, nq),
        in_specs=[q_spec(QK_W), q_spec(DQ_W), q_spec(DQ_W), k_spec(QK_W), kt_spec(MLA_KV_RANK), k_spec(DK_W),
                  kt_spec(DV_W), _const_spec(wuvt.shape), _const_spec(gsub_col.shape)]
                 + [_const_spec(l.shape) for l in lams],
        out_specs=q_spec(mix_w),
        out_shape=jax.ShapeDtypeStruct((b, s, mix_w), BF16),
        scratch_shapes=[pltpu.VMEM((ATTN_MAPS, 1, tq), F32), pltpu.VMEM((ATTN_MAPS, 1, tq), F32),
                        pltpu.VMEM((ATTN_MAPS, DIFF_VD, tq), F32), pltpu.VMEM((ATTN_SLOTS, tq, tq), F32)],
        compiler_params=_cparams(("parallel", "parallel", "arbitrary")),
        name="attn",
    )(qf, dq0, dq1, kf, ckvt, dkb, dvt, wuvt, gsub_col, *lams)


def _qabs_kernel(qf_ref, gk_ref, wt_ref, qa_ref):
    for h in range(MLA_HEADS):
        sl = slice(h * LANES, (h + 1) * LANES)
        qg = (qf_ref[:, sl].astype(F32) * gk_ref[...]).astype(BF16)
        qa_ref[:, sl] = _dot(qg, wt_ref[h])


def _qabs(qf, gk_row, wuk_t):
    n = qf.shape[0]
    return pl.pallas_call(
        _qabs_kernel,
        grid=(1,),
        in_specs=[_const_spec(qf.shape), _const_spec(gk_row.shape), _const_spec(wuk_t.shape)],
        out_specs=_const_spec((n, QK_W)),
        out_shape=jax.ShapeDtypeStruct((n, QK_W), F32),
        compiler_params=_cparams(("arbitrary",)),
        name="qabs",
    )(qf, gk_row, wuk_t)


def _dec_pipeline(refs, *, layer, n_seq, n_pages, lam_init):
    (pt_ref, qa_ref, qp_ref, qbd_ref, qf8_ref, kf8_ref, dkrow_ref, ckvrow_ref, dvrow_ref,
     wukt_ref, wuv_ref, gsub_ref, lq1_ref, lk1_ref, lq2_ref, lk2_ref,
     ckv_hbm, kpe_hbm, kt_hbm, v_hbm, oa_ref, ob_ref,
     ckv_buf, kpe_buf, kt_buf, v_buf, sems, lhs_ref, m_a, l_a, acc_a, m_d, l_d, acc_d) = refs
    pp = DEC_PAGES
    n_steps = n_pages // pp
    total = n_seq * n_steps
    hbm = (ckv_hbm, kpe_hbm, kt_hbm, v_hbm)
    bufs = (ckv_buf, kpe_buf, kt_buf, v_buf)
    nk = MLA_HEADS * MLA_NOPE
    n_stages = DEC_CHAINS + DEC_SKEW

    def page_copy(a, slot, k, page):
        return pltpu.make_async_copy(hbm[a].at[layer, page], bufs[a].at[slot, k], sems.at[a, slot])

    def start_step(t, slot):
        b = t // n_steps
        first = (t - b * n_steps) * pp
        for k in range(pp):
            page = pt_ref[b, first + k]
            for a in range(len(hbm)):
                page_copy(a, slot, k, page).start()

    def wait_step(slot):
        for k in range(pp):
            for a in range(len(hbm)):
                page_copy(a, slot, k, 0).wait()

    def update(s, m_ref, l_ref, c):
        m_prev = m_ref[c]
        m_new = jnp.maximum(m_prev, jnp.max(s, axis=-1, keepdims=True))
        alpha = jnp.exp2(m_prev - m_new)
        p = jnp.exp2(s - m_new)
        l_ref[c] = alpha * l_ref[c] + jnp.sum(p, axis=-1, keepdims=True)
        m_ref[c] = m_new
        return alpha, p

    def pages(c):
        return range(c * DEC_CHAIN_PAGES, (c + 1) * DEC_CHAIN_PAGES)

    def stages(t):
        slot = t & 1
        b = t // n_steps
        j = t - b * n_steps
        row = lax.broadcasted_iota(jnp.int32, (MLA_HEADS, 1), 0)
        row_g0 = ((row >> 1) & 1) == 0
        prods = {}

        def begin():
            start_step(jnp.minimum(t + 1, total - 1), 1 - slot)
            wait_step(slot)
            lhs_ref[nk:, :] = jnp.concatenate(
                [qa_ref[b].astype(BF16), jnp.zeros((lhs_ref.shape[0] - nk - MLA_HEADS, LANES), BF16)], axis=0)
            for r in (m_a, m_d):
                r[...] = jnp.where(j == 0, NEG, r[...])
            for r in (l_a, acc_a, l_d, acc_d):
                r[...] = jnp.where(j == 0, 0.0, r[...])

        def score_products(c):
            ckv = jnp.concatenate([ckv_buf[slot, k] for k in pages(c)], axis=0).astype(BF16)
            res = _dot_nt(lhs_ref[...], ckv)
            kpe_t = jnp.concatenate([kpe_buf[slot, k] for k in pages(c)], axis=1).astype(BF16)
            bp = _dot(qp_ref[b], kpe_t)
            kt = jnp.concatenate([kt_buf[slot, k] for k in pages(c)], axis=1).astype(BF16)
            return ckv, res, bp, _dot(qbd_ref[b], kt)

        def softmax(c, res, bp, sd):
            sq = res[:nk] * res[:nk]
            ssq = jnp.concatenate(
                [jnp.sum(sq[h * MLA_NOPE:(h + 1) * MLA_NOPE], axis=0, keepdims=True) for h in range(MLA_HEADS)],
                axis=0)
            rnorm = lax.rsqrt(ssq * (1.0 / MLA_NOPE) + EPS)
            alpha_a, p_a = update(res[nk:nk + MLA_HEADS] * rnorm + bp, m_a, l_a, c)
            alpha_d, p_d = update(sd, m_d, l_d, c)
            return alpha_a, p_a.astype(BF16), alpha_d, p_d.astype(BF16)

        def value_products(c, ckv, alpha_a, p_a, alpha_d, p_d):
            acc_a[c] = alpha_a * acc_a[c] + _dot(p_a, ckv)
            pv = []
            for g in range(DIFF_KV_HEADS):
                v = jnp.concatenate(
                    [v_buf[slot, k, pl.ds(g, PAGE_SIZE, stride=DIFF_KV_HEADS), :] for k in pages(c)], axis=0)
                pv.append(_dot(p_d, v.astype(BF16)))
            acc_d[c] = alpha_d * acc_d[c] + jnp.where(row_g0, pv[0], pv[1])

        def finish():
            def merged(s_self, v_self, m_ref, l_ref, acc_ref):
                m = s_self
                for c in range(DEC_CHAINS):
                    m = jnp.maximum(m, m_ref[c])
                p_self = jnp.exp2(s_self - m)
                l = p_self
                acc = p_self * v_self
                for c in range(DEC_CHAINS):
                    w = jnp.exp2(m_ref[c] - m)
                    l = l + w * l_ref[c]
                    acc = acc + w * acc_ref[c]
                return acc / l

            s_self = jnp.sum(qf8_ref[b].astype(F32) * kf8_ref[b].astype(F32), axis=-1, keepdims=True)
            lat = merged(s_self, ckvrow_ref[b], m_a, l_a, acc_a)
            full = _dot(lat.astype(BF16), wuv_ref[...])
            col_head = lax.broadcasted_iota(jnp.int32, full.shape, 1) >> 6
            row_head = lax.broadcasted_iota(jnp.int32, full.shape, 0)
            oa_ref[b] = jnp.sum(jnp.where(col_head == row_head, full, 0.0), axis=0, keepdims=True)
            s_self = jnp.sum(qbd_ref[b].astype(F32) * dkrow_ref[b].astype(F32), axis=-1, keepdims=True)
            dvrow = dvrow_ref[b]
            v_self = jnp.where(row_g0, dvrow[:, :DIFF_VD], dvrow[:, DIFF_VD:])
            o = merged(s_self, v_self, m_d, l_d, acc_d)
            lam = _lam(lq1_ref, lk1_ref, lq2_ref, lk2_ref, lam_init)
            d = o[:DIFF_HEADS] - lam * o[DIFF_HEADS:]
            ob_ref[b] = _rms(d, gsub_ref[...]) * (1.0 - lam_init)

        def make(step):
            def run():
                if step == 0:
                    begin()
                if step < DEC_CHAINS:
                    prods[step] = score_products(step)
                c = step - DEC_SKEW
                if c >= 0:
                    ckv, res, bp, sd = prods.pop(c)
                    value_products(c, ckv, *softmax(c, res, bp, sd))
                if step == n_stages - 1:
                    finish()
            return run

        return [make(step) for step in range(n_stages)]

    def prime():
        lhs_ref[:nk, :] = wukt_ref[...]
        start_step(0, 0)

    def drain():
        wait_step(total & 1)

    return prime, drain, stages, total


def _dec_scratch(caches):
    return ([pltpu.VMEM((2, DEC_PAGES) + c.shape[2:], c.dtype) for c in caches]
            + [pltpu.SemaphoreType.DMA((len(caches), 2)),
               pltpu.VMEM((MLA_HEADS * MLA_NOPE + 16, LANES), BF16),
               pltpu.VMEM((DEC_CHAINS, MLA_HEADS, 1), F32), pltpu.VMEM((DEC_CHAINS, MLA_HEADS, 1), F32),
               pltpu.VMEM((DEC_CHAINS, MLA_HEADS, MLA_KV_RANK), F32),
               pltpu.VMEM((DEC_CHAINS, MLA_HEADS, 1), F32), pltpu.VMEM((DEC_CHAINS, MLA_HEADS, 1), F32),
               pltpu.VMEM((DEC_CHAINS, MLA_HEADS, DIFF_VD), F32)])


def _dec_kernel(*refs, layer, n_seq, n_pages, lam_init):
    prime, drain, stages, total = _dec_pipeline(
        refs, layer=layer, n_seq=n_seq, n_pages=n_pages, lam_init=lam_init)

    def body(t, carry):
        for run in stages(t):
            run()
        return carry

    prime()
    lax.fori_loop(0, total, body, 0)
    drain()


def _decode(page_table, per_seq, consts, caches, layer, lam_init):
    b, n_pages = page_table.shape

    def full_spec(a):
        return pl.BlockSpec(a.shape, lambda i, pt: (0,) * a.ndim)

    oa_w = MLA_HEADS * MLA_V
    out_shape = [jax.ShapeDtypeStruct((b, 1, oa_w), F32), jax.ShapeDtypeStruct((b, DIFF_HEADS, DIFF_VD), F32)]
    grid_spec = pltpu.PrefetchScalarGridSpec(
        num_scalar_prefetch=1,
        grid=(1,),
        in_specs=[full_spec(a) for a in list(per_seq) + list(consts)]
                 + [pl.BlockSpec(memory_space=pl.ANY)] * len(caches),
        out_specs=[full_spec(o) for o in out_shape],
        scratch_shapes=_dec_scratch(caches),
    )
    return pl.pallas_call(
        functools.partial(_dec_kernel, layer=layer, n_seq=b, n_pages=n_pages, lam_init=lam_init),
        grid_spec=grid_spec,
        out_shape=out_shape,
        compiler_params=_cparams(("arbitrary",)),
        name="decode",
    )(page_table, *per_seq, *consts, *caches)


def _out_kernel(mix_ref, x_ref, gt_ref, sc_ref, sh_ref, wo_ref, g2_ref, wrh_ref, wrl_ref, br_ref,
                x1_ref, h2_ref, gate_ref):
    o = _dot(mix_ref[...], wo_ref[...])
    x1 = x_ref[...] + gt_ref[...] * o
    x1_ref[...] = x1
    h2 = _rms(x1, g2_ref[...]) * (1.0 + sc_ref[...]) + sh_ref[...]
    h2_ref[...] = h2.astype(BF16)
    hh, hl = _split(h2)
    logits = _dot(hh, wrh_ref[...]) + _dot(hh, wrl_ref[...]) + _dot(hl, wrh_ref[...]) + br_ref[...]
    lane_i = lax.broadcasted_iota(jnp.int32, logits.shape, 1)
    lane = lane_i.astype(F32)
    big = float(ROUTER_LANES)
    gl = jnp.where(lane_i < N_GROUPS, logits, NEG)
    gmax = jnp.max(gl, axis=-1, keepdims=True)
    gidx = jnp.min(jnp.where(gl == gmax, lane, big), axis=-1, keepdims=True)
    g_w = 1.0 / jnp.sum(jnp.exp(gl - gmax), axis=-1, keepdims=True)
    in_group = (lane_i >= N_GROUPS) & (lane_i < N_GROUPS + N_ROUTED) & (
        ((lane_i - N_GROUPS) >> 3).astype(F32) == gidx)
    el = jnp.where(in_group, logits, NEG)
    e1 = jnp.max(el, axis=-1, keepdims=True)
    i1 = jnp.min(jnp.where(el == e1, lane, big), axis=-1, keepdims=True)
    el2 = jnp.where(lane == i1, NEG, el)
    e2 = jnp.max(el2, axis=-1, keepdims=True)
    i2 = jnp.min(jnp.where(el2 == e2, lane, big), axis=-1, keepdims=True)
    t = jnp.exp(e2 - e1)
    w1 = 1.0 / (1.0 + t)
    w2 = t / (1.0 + t)
    gate_ref[...] = jnp.where(lane == i1, w1, jnp.where(lane == i2, w2, 0.0)) * g_w


def _out(mix, x3, gt, sc, sh, wo, g2, wrh, wrl, br, tm):
    b, t, d = x3.shape
    per_tok = gt.shape[1] != 1

    def tok_spec(w):
        return pl.BlockSpec((None, tm, w), lambda s, bb: (bb, s, 0))

    mod_spec = tok_spec(d) if per_tok else pl.BlockSpec((None, 1, d), lambda s, bb: (bb, 0, 0))
    consts = [wo, g2, wrh, wrl, br]
    return pl.pallas_call(
        _out_kernel,
        grid=(t // tm, b),
        in_specs=[tok_spec(mix.shape[2]), tok_spec(d), mod_spec, mod_spec, mod_spec] + [_const_spec(c.shape) for c in consts],
        out_specs=[tok_spec(d), tok_spec(d), tok_spec(ROUTER_LANES)],
        out_shape=[jax.ShapeDtypeStruct((b, t, d), F32), jax.ShapeDtypeStruct((b, t, d), BF16),
                   jax.ShapeDtypeStruct((b, t, ROUTER_LANES), F32)],
        compiler_params=_cparams(("parallel", "parallel")),
        name="out",
    )(mix, x3, gt, sc, sh, *consts)


def _moe_kernel(h2_ref, gate_ref, x1_ref, gt_ref, wg_ref, wu_ref, wd_ref, ex_ref, y_ref, acc_ref):
    g = pl.program_id(2)

    @pl.when(g == 0)
    def _():
        acc_ref[...] = jnp.zeros(acc_ref.shape, F32)

    h = h2_ref[...]
    a = _silu(_dot(h, wg_ref[...])) * _dot(h, wu_ref[...])
    ge = _dot(gate_ref[...].astype(BF16), ex_ref[...])
    acc_ref[...] += _dot((a * ge).astype(BF16), wd_ref[...])

    @pl.when(g == N_GROUPS - 1)
    def _():
        y_ref[...] = x1_ref[...] + gt_ref[...] * acc_ref[...]


def _moe(h2, gate, x1, gt, wg, wu, wd, ex, tm):
    b, t, d = x1.shape
    per_tok = gt.shape[1] != 1

    def tok_spec(w):
        return pl.BlockSpec((None, tm, w), lambda s, bb, g: (bb, s, 0))

    mod_spec = tok_spec(d) if per_tok else pl.BlockSpec((None, 1, d), lambda s, bb, g: (bb, 0, 0))

    def grp_spec(a):
        return pl.BlockSpec((None,) + a.shape[1:], lambda s, bb, g: (g, 0, 0))

    return pl.pallas_call(
        _moe_kernel,
        grid=(t // tm, b, N_GROUPS),
        in_specs=[tok_spec(d), tok_spec(ROUTER_LANES), tok_spec(d), mod_spec,
                  grp_spec(wg), grp_spec(wu), grp_spec(wd), grp_spec(ex)],
        out_specs=tok_spec(d),
        out_shape=jax.ShapeDtypeStruct((b, t, d), F32),
        scratch_shapes=[pltpu.VMEM((tm, d), F32)],
        compiler_params=_cparams(("parallel", "parallel", "arbitrary")),
        name="moe",
    )(h2, gate, x1, gt, wg, wu, wd, ex)


MOE_CHUNK = MXU_DIM


def _moe_dec_kernel(pt_ref, h2_ref, gate_ref, x1_ref, gt_ref, wg_ref, wu_ref, wd_ref, ex_ref, *rest,
                    dec_per_step, layer, n_seq, n_pages, lam_init):
    n_dec_in = 19
    dec_in, (y_ref, oa_ref, ob_ref, acc_ref), dec_scr = rest[:n_dec_in], rest[n_dec_in:n_dec_in + 4], rest[n_dec_in + 4:]
    prime, drain, stages, total = _dec_pipeline(
        (pt_ref,) + tuple(dec_in) + (oa_ref, ob_ref) + tuple(dec_scr),
        layer=layer, n_seq=n_seq, n_pages=n_pages, lam_init=lam_init)
    g = pl.program_id(2)
    q = (pl.program_id(0) * pl.num_programs(1) + pl.program_id(1)) * N_GROUPS + g
    n_q = pl.num_programs(0) * pl.num_programs(1) * N_GROUPS

    @pl.when(q == 0)
    def _():
        prime()

    h = h2_ref[...]
    gate_b = gate_ref[...].astype(BF16)
    up, act = {}, {}

    def moe_up(c):
        cs = slice(c * MOE_CHUNK, (c + 1) * MOE_CHUNK)
        up[c] = _dot(h, wu_ref[:, cs]) * _dot(gate_b, ex_ref[:, cs])

    def moe_gate(c):
        cs = slice(c * MOE_CHUNK, (c + 1) * MOE_CHUNK)
        act[c] = _silu(_dot(h, wg_ref[:, cs]))

    def moe_down(c):
        cs = slice(c * MOE_CHUNK, (c + 1) * MOE_CHUNK)
        part = _dot((act.pop(c) * up.pop(c)).astype(BF16), wd_ref[cs, :])
        acc_ref[...] = (jnp.where(g == 0, 0.0, acc_ref[...]) if c == 0 else acc_ref[...]) + part

    n_chunks = GROUP_FF // MOE_CHUNK
    moe_ops = []
    for c in range(n_chunks + 1):
        if c < n_chunks:
            moe_ops += [functools.partial(moe_up, c), functools.partial(moe_gate, c)]
        if c >= 1:
            moe_ops.append(functools.partial(moe_down, c - 1))

    dec_ops = []
    for u in range(dec_per_step):
        dec_ops += stages(q * dec_per_step + u)

    done = 0
    for n, run in enumerate(dec_ops):
        run()
        target = (n + 1) * len(moe_ops) // len(dec_ops)
        while done < target:
            moe_ops[done]()
            done += 1
    while done < len(moe_ops):
        moe_ops[done]()
        done += 1

    y_ref[...] = x1_ref[...] + gt_ref[...] * acc_ref[...]

    @pl.when(q == n_q - 1)
    def _():
        drain()


def _moe_dec(h2, gate, x1, gt, wg, wu, wd, ex, tm, page_table, per_seq, consts, caches, layer, lam_init):
    b, t, d = x1.shape
    n_seq, n_pages = page_table.shape
    n_q = (t // tm) * b * N_GROUPS
    total = n_seq * (n_pages // DEC_PAGES)
    assert total % n_q == 0 and gt.shape[1] == 1

    def tok_spec(w):
        return pl.BlockSpec((None, tm, w), lambda s, bb, g, pt: (bb, s, 0))

    def grp_spec(a):
        return pl.BlockSpec((None,) + a.shape[1:], lambda s, bb, g, pt: (g, 0, 0))

    def full_spec(a):
        return pl.BlockSpec(a.shape, lambda s, bb, g, pt: (0,) * a.ndim)

    oa_w = MLA_HEADS * MLA_V
    dec_out = [jax.ShapeDtypeStruct((n_seq, 1, oa_w), F32), jax.ShapeDtypeStruct((n_seq, DIFF_HEADS, DIFF_VD), F32)]
    grid_spec = pltpu.PrefetchScalarGridSpec(
        num_scalar_prefetch=1,
        grid=(t // tm, b, N_GROUPS),
        in_specs=[tok_spec(d), tok_spec(ROUTER_LANES), tok_spec(d),
                  pl.BlockSpec((None, 1, d), lambda s, bb, g, pt: (bb, 0, 0)),
                  grp_spec(wg), grp_spec(wu), grp_spec(wd), grp_spec(ex)]
                 + [full_spec(a) for a in list(per_seq) + list(consts)]
                 + [pl.BlockSpec(memory_space=pl.ANY)] * len(caches),
        out_specs=[tok_spec(d)] + [full_spec(o) for o in dec_out],
        scratch_shapes=[pltpu.VMEM((tm, d), F32)] + _dec_scratch(caches),
    )
    return pl.pallas_call(
        functools.partial(_moe_dec_kernel, dec_per_step=total // n_q, layer=layer, n_seq=n_seq, n_pages=n_pages,
                          lam_init=lam_init),
        grid_spec=grid_spec,
        out_shape=[jax.ShapeDtypeStruct((b, t, d), F32)] + dec_out,
        compiler_params=_cparams(("arbitrary", "arbitrary", "arbitrary")),
        name="moe_decode",
    )(page_table, h2, gate, x1, gt, wg, wu, wd, ex, *per_seq, *consts, *caches)


def _rope_tables(pos):
    def cs(dim):
        half = dim // 2
        inv = ROPE_THETA ** (-jnp.arange(half, dtype=F32) * 2.0 / dim)
        ang = pos[:, None] * inv[None, :]
        c, s = jnp.cos(ang), jnp.sin(ang)
        return jnp.concatenate([c, c], axis=1), jnp.concatenate([-s, s], axis=1)

    t = pos.shape[0]
    c32, s32 = cs(MLA_ROPE)
    c64, s64 = cs(DIFF_HD)
    z = lambda w: jnp.zeros((t, w), F32)
    pad = LANES - MLA_NOPE - MLA_ROPE
    cosq = jnp.concatenate([jnp.ones((t, MLA_NOPE), F32), c32, z(pad)], axis=1)
    sinq = jnp.concatenate([z(MLA_NOPE), s32, z(pad)], axis=1)
    cosk = jnp.concatenate([c32, z(LANES - MLA_ROPE)], axis=1)
    sink = jnp.concatenate([s32, z(LANES - MLA_ROPE)], axis=1)
    cosd = jnp.concatenate([c64, c64], axis=1)
    sind = jnp.concatenate([s64, s64], axis=1)
    return [cosq, sinq, cosk, sink, cosd, sind]


def _block_diag_mean(sizes, width):
    m = jnp.zeros((width, width), F32)
    o = 0
    while o < width:
        for sz in sizes:
            if sz > 0:
                m = m.at[o:o + sz, o:o + sz].set(1.0 / sz)
            o += abs(sz)
    return m.astype(BF16)


def _layer_weights(l, w_in, g_norm1, g_mla_qa, w_mla_uq, g_mla_kva, w_mla_uk, g_mla_qn_nope, g_mla_qn_rope,
                   g_mla_kn_nope, g_mla_kn_rope, g_diff_qn, g_diff_kn):
    d = w_in.shape[1]
    o_kpe = MLA_Q_RANK + MLA_KV_RANK
    wi = w_in[l]
    win = jnp.concatenate([wi[:, :o_kpe], wi[:, o_kpe:o_kpe + MLA_ROPE], jnp.zeros((d, LANES - MLA_ROPE), F32),
                           wi[:, o_kpe + MLA_ROPE:]], axis=1).astype(BF16)
    pad = LANES - MLA_NOPE - MLA_ROPE
    wuq = w_mla_uq[l].reshape(MLA_Q_RANK, MLA_HEADS, MLA_NOPE + MLA_ROPE)
    wuq = jnp.concatenate([wuq, jnp.zeros((MLA_Q_RANK, MLA_HEADS, pad), F32)], axis=2).reshape(MLA_Q_RANK, QK_W).astype(BF16)
    wuk = jnp.concatenate([w_mla_uk[l], jnp.zeros((MLA_KV_RANK, MLA_HEADS, LANES - MLA_NOPE), F32)], axis=2)
    wuk = wuk.reshape(MLA_KV_RANK, QK_W).astype(BF16)
    gq = jnp.tile(jnp.concatenate([g_mla_qn_nope[l], g_mla_qn_rope[l], jnp.zeros((pad,), F32)]), MLA_HEADS)[None]
    gk = jnp.tile(jnp.concatenate([g_mla_kn_nope[l], jnp.zeros((LANES - MLA_NOPE,), F32)]), MLA_HEADS)[None]
    gkpe = jnp.concatenate([g_mla_kn_rope[l], jnp.zeros((LANES - MLA_ROPE,), F32)])[None]
    gdq = jnp.tile(g_diff_qn[l], DQ_W // DIFF_HD)[None]
    gdk = jnp.tile(g_diff_kn[l], DK_W // DIFF_HD)[None]
    bdq = _block_diag_mean((MLA_NOPE, MLA_ROPE, -pad), MXU_DIM)
    bdd = _block_diag_mean((DIFF_HD,), MXU_DIM)
    return [g_norm1[l][None], win, g_mla_qa[l][None], wuq, gq, g_mla_kva[l][None], wuk, gk, gkpe, gdq, gdk, bdq, bdd]


def kernel(x_prompt, x_sample, cache_mla_ckv, cache_mla_kpe, cache_diff_k, cache_diff_v, page_table, c_prompt, c_sample, w_ada, b_ada, g_norm1, w_in, g_mla_qa, w_mla_uq, g_mla_kva, w_mla_uk, w_mla_uv, g_mla_qn_nope, g_mla_qn_rope, g_mla_kn_nope, g_mla_kn_rope, g_diff_qn, g_diff_kn, lam_q1, lam_k1, lam_q2, lam_k2, g_diff_subln, w_o, g_norm2, w_router_group, b_router_group, w_router_expert, b_router_expert, w_exp_gate, w_exp_up, w_exp_down):
    bp, sp, d = x_prompt.shape
    bs, ts, _ = x_sample.shape
    depth = w_in.shape[0]
    n_pool = cache_mla_ckv.shape[1]
    n_pages = page_table.shape[1]
    assert ts == 1 and n_pages % DEC_PAGES == 0 and cache_mla_ckv.shape[2] == PAGE_SIZE
    assert w_in.shape[2] == PROJ_W - LANES + MLA_ROPE and d % MXU_DIM == 0
    past = n_pages * PAGE_SIZE

    tm_p = min(256, sp)
    tq = min(512, sp)
    tm_o = min(512, sp)
    tables_p = _rope_tables(jnp.arange(sp, dtype=F32))
    tables_s = _rope_tables(jnp.arange(ts, dtype=F32) + past)

    kpe_t = jnp.transpose(cache_mla_kpe, (0, 1, 3, 2))
    k_t = jnp.transpose(cache_diff_k, (0, 1, 3, 4, 5, 2)).reshape(depth, n_pool, DK_W, PAGE_SIZE)
    v_rows = cache_diff_v.reshape(depth, n_pool, PAGE_SIZE * DIFF_KV_HEADS, DIFF_VD)

    xp = x_prompt
    xs = x_sample.reshape(1, bs, d)
    outs_p = [[], [], [], []]
    outs_s = [[], [], [], []]
    for l in range(depth):
        lam_init = 0.8 - 0.6 * math.exp(-0.3 * l)
        wts = _layer_weights(l, w_in, g_norm1, g_mla_qa, w_mla_uq, g_mla_kva, w_mla_uk, g_mla_qn_nope,
                             g_mla_qn_rope, g_mla_kn_nope, g_mla_kn_rope, g_diff_qn, g_diff_kn)
        lams = [lam_q1[l][None], lam_k1[l][None], lam_q2[l][None], lam_k2[l][None]]
        gsub = g_diff_subln[l][None]
        wuv = w_mla_uv[l]
        wuv_flat = wuv.reshape(MLA_KV_RANK, MLA_HEADS * MLA_V).astype(BF16)
        wuvt = jnp.transpose(wuv, (1, 2, 0)).astype(BF16)
        wuk_t = jnp.concatenate([jnp.transpose(w_mla_uk[l], (1, 2, 0)),
                                 jnp.zeros((MLA_HEADS, LANES - MLA_NOPE, MLA_KV_RANK), F32)], axis=1).astype(BF16)
        wukt_rows = jnp.transpose(w_mla_uk[l], (1, 2, 0)).reshape(MLA_HEADS * MLA_NOPE, MLA_KV_RANK).astype(BF16)
        gk_row = jnp.concatenate([g_mla_kn_nope[l], jnp.zeros((LANES - MLA_NOPE,), F32)])[None]
        wo = w_o[l].astype(BF16)
        wr = jnp.concatenate([w_router_group[l], jnp.transpose(w_router_expert[l], (1, 0, 2)).reshape(d, N_ROUTED),
                              jnp.zeros((d, ROUTER_LANES - N_GROUPS - N_ROUTED), F32)], axis=1)
        wrh, wrl = _split(wr)
        br = jnp.concatenate([b_router_group[l], b_router_expert[l].reshape(N_ROUTED),
                              jnp.zeros((ROUTER_LANES - N_GROUPS - N_ROUTED,), F32)])[None]
        wg = jnp.transpose(w_exp_gate[l], (0, 2, 1, 3)).reshape(N_GROUPS, d, GROUP_FF).astype(BF16)
        wu = jnp.transpose(w_exp_up[l], (0, 2, 1, 3)).reshape(N_GROUPS, d, GROUP_FF).astype(BF16)
        wd = w_exp_down[l].reshape(N_GROUPS, GROUP_FF, d).astype(BF16)
        lane = jnp.arange(ROUTER_LANES)[None, :, None]
        col = jnp.arange(GROUP_FF)[None, None, :]
        grp = jnp.arange(N_GROUPS)[:, None, None]
        ex = (lane == N_GROUPS + grp * EXPERTS_PER_GROUP + col // EXPERT_FF).astype(BF16)

        mod = _ada(jnp.concatenate([c_prompt, c_sample], axis=0), w_ada[l], b_ada[l][None])
        mod_p = mod[:bp].reshape(bp, 6, 1, d)
        mod_s = mod[bp:].reshape(1, bs, 6, d)
        sh1p, sc1p, gt1p, sh2p, sc2p, gt2p = [mod_p[:, k] for k in range(6)]
        sh1s, sc1s, gt1s, sh2s, sc2s, gt2s = [mod_s[:, :, k] for k in range(6)]

        (qf, kf, dq0, dq1, ckvt, dkb, dvt, ckv, kpe, dk, dv) = _proj(xp, sc1p, sh1p, tables_p, wts, tm_p)
        mix = _attn(qf, dq0, dq1, kf, ckvt, dkb, dvt, wuvt, gsub.reshape(DIFF_VD, 1), lams, lam_init, tq)
        x1, h2, gate = _out(mix, xp, gt1p, sc2p, sh2p, wo, g_norm2[l][None], wrh, wrl, br, tm_o)
        for lst, a in zip(outs_p, (ckv, kpe, dk, dv)):
            lst.append(a)
        (qf, kf, dq0, dq1, _, dkb, _, ckv, kpe, dk, dv) = _proj(xs, sc1s, sh1s, tables_s, wts, bs)
        for lst, a in zip(outs_s, (ckv, kpe, dk, dv)):
            lst.append(a)
        qf2 = qf.reshape(bs, QK_W)
        qa = _qabs(qf2, gk_row, wuk_t).reshape(bs, MLA_HEADS, LANES)
        qf8 = qf2.reshape(bs, MLA_HEADS, LANES)
        kf8 = kf.reshape(bs, MLA_HEADS, LANES)
        qp = qf8[:, :, MLA_NOPE:MLA_NOPE + MLA_ROPE]
        dq5 = (dq0 + dq1).reshape(bs, DIFF_KV_HEADS, 2, 2, DIFF_HD)
        eye = jnp.eye(2, dtype=BF16)
        qbd = jnp.einsum('bgrmd,gh,mn->bmgrhnd', dq5, eye, eye).reshape(bs, 2 * DIFF_HEADS, DK_W)
        per_seq = [qa, qp, qbd, qf8, kf8, dkb.reshape(bs, 1, DK_W), ckv.reshape(bs, 1, MLA_KV_RANK),
                   dv.reshape(bs, 1, DV_W)]
        consts = [wukt_rows, wuv_flat, gsub] + lams
        caches = (cache_mla_ckv, kpe_t, k_t, v_rows)

        moe_steps = (sp // tm_o) * bp * N_GROUPS
        if (bs * (n_pages // DEC_PAGES)) % moe_steps == 0:
            xp, oa, ob = _moe_dec(h2, gate, x1, gt2p, wg, wu, wd, ex, tm_o, page_table, per_seq, consts, caches,
                                  l, lam_init)
        else:
            xp = _moe(h2, gate, x1, gt2p, wg, wu, wd, ex, tm_o)
            oa, ob = _decode(page_table, per_seq, consts, caches, l, lam_init)

        mix = jnp.concatenate([oa.reshape(bs, -1), ob.reshape(bs, -1)], axis=1).astype(BF16).reshape(1, bs, -1)
        x1, h2, gate = _out(mix, xs, gt1s, sc2s, sh2s, wo, g_norm2[l][None], wrh, wrl, br, bs)
        xs = _moe(h2, gate, x1, gt2s, wg, wu, wd, ex, bs)

    def stack_p(lst, tail):
        return jnp.stack(lst).reshape((depth, bp, sp) + tail)

    def stack_s(lst, tail):
        return jnp.stack(lst).reshape((depth, bs, ts) + tail)

    k_tail = (DIFF_KV_HEADS, 2, DIFF_HD)
    v_tail = (DIFF_KV_HEADS, DIFF_VD)
    return (xp, xs.reshape(bs, ts, d),
            stack_p(outs_p[0], (MLA_KV_RANK,)), stack_p(outs_p[1], (MLA_ROPE,)), stack_p(outs_p[2], k_tail),
            stack_p(outs_p[3], v_tail),
            stack_s(outs_s[0], (MLA_KV_RANK,)), stack_s(outs_s[1], (MLA_ROPE,)), stack_s(outs_s[2], k_tail),
            stack_s(outs_s[3], v_tail))
```

```python
import functools
import math

import jax
import jax.numpy as jnp
from jax import lax
from jax.experimental import pallas as pl
from jax.experimental.pallas import tpu as pltpu

F32 = jnp.float32
BF16 = jnp.bfloat16

MLA_HEADS = 8
MLA_Q_RANK = 256
MLA_KV_RANK = 128
MLA_NOPE = 64
MLA_ROPE = 32
MLA_V = 64
DIFF_HEADS = 4
DIFF_KV_HEADS = 2
DIFF_HD = 64
DIFF_VD = 128
N_GROUPS = 4
EXPERTS_PER_GROUP = 8
EXPERT_FF = 128
PAGE_SIZE = 128
ROPE_THETA = 10000.0
EPS = 1e-6
LOG2E = 1.4426950408889634
MLA_SCALE = (MLA_NOPE + MLA_ROPE) ** -0.5
DIFF_SCALE = DIFF_HD ** -0.5
NEG = -1e30

LANES = 128
MXU_DIM = 256
VMEM_LIMIT = 56 * 1024 * 1024
ROUTER_LANES = 128
N_ROUTED = N_GROUPS * EXPERTS_PER_GROUP
GROUP_FF = EXPERTS_PER_GROUP * EXPERT_FF
QK_W = MLA_HEADS * LANES
DQ_W = DIFF_HEADS * 2 * DIFF_HD
DK_W = DIFF_KV_HEADS * 2 * DIFF_HD
DV_W = DIFF_KV_HEADS * DIFF_VD
PROJ_W = MLA_Q_RANK + MLA_KV_RANK + LANES + DQ_W + DK_W + DV_W
DEC_PAGES = 16
DEC_CHAIN_PAGES = 2
DEC_CHAINS = DEC_PAGES // DEC_CHAIN_PAGES
DEC_SKEW = 2


def _dot(a, b):
    return jnp.dot(a, b, preferred_element_type=F32)


def _dot_nt(a, b):
    return lax.dot_general(a, b, (((1,), (1,)), ((), ())), preferred_element_type=F32)


def _split(a):
    hi = a.astype(BF16)
    lo = (a - hi.astype(F32)).astype(BF16)
    return hi, lo


def _rms(v, g):
    return v * lax.rsqrt(jnp.mean(v * v, axis=-1, keepdims=True) + EPS) * g


def _silu(v):
    return v / (1.0 + jnp.exp(-v))


def _cparams(sem):
    return pltpu.CompilerParams(dimension_semantics=sem, vmem_limit_bytes=VMEM_LIMIT)


def _const_spec(shape):
    nd = len(shape)
    return pl.BlockSpec(shape, lambda *_: (0,) * nd)


def _ada_kernel(c_ref, w_ref, b_ref, o_ref):
    s = _silu(c_ref[...])
    sh, sl = _split(s)
    wh, wl = _split(w_ref[...])
    o_ref[...] = _dot(sh, wh) + _dot(sh, wl) + _dot(sl, wh) + b_ref[...]


def _ada(c, w, b):
    m, d = c.shape
    n = w.shape[1]
    tn = 512
    return pl.pallas_call(
        _ada_kernel,
        grid=(n // tn,),
        in_specs=[_const_spec((m, d)), pl.BlockSpec((d, tn), lambda i: (0, i)), pl.BlockSpec((1, tn), lambda i: (0, i))],
        out_specs=pl.BlockSpec((m, tn), lambda i: (0, i)),
        out_shape=jax.ShapeDtypeStruct((m, n), F32),
        compiler_params=_cparams(("parallel",)),
        name="ada",
    )(c, w, b)


def _block_norm(v, bd, g):
    w = v.shape[1]
    sq = (v * v).astype(BF16)
    ms = jnp.concatenate([_dot(sq[:, i:i + MXU_DIM], bd) for i in range(0, w, MXU_DIM)], axis=1)
    return v * lax.rsqrt(ms + EPS) * g


def _rope(v, cos, sin, half, first):
    parts = []
    for i in range(0, v.shape[1], LANES):
        s = v[:, i:i + LANES]
        rot = jnp.where(first, pltpu.roll(s, LANES - half, 1), pltpu.roll(s, half, 1))
        parts.append(s * cos + rot * sin)
    return parts[0] if len(parts) == 1 else jnp.concatenate(parts, axis=1)


def _proj_kernel(x_ref, sc_ref, sh_ref, cosq_ref, sinq_ref, cosk_ref, sink_ref, cosd_ref, sind_ref,
                 g1_ref, win_ref, gqa_ref, wuq_ref, gq_ref, gkva_ref, wuk_ref, gk_ref, gkpe_ref, gdq_ref, gdk_ref,
                 bdq_ref, bdd_ref,
                 qf_ref, kf_ref, dq0_ref, dq1_ref, ckvt_ref, dkb_ref, dvt_ref, ckv_ref, kpe_ref, dk_ref, dv_ref):
    lane = lax.broadcasted_iota(jnp.int32, (1, LANES), 1)
    x = x_ref[...]
    h = _rms(x, g1_ref[...]) * (1.0 + sc_ref[...]) + sh_ref[...]
    proj = _dot(h.astype(BF16), win_ref[...])
    o_ckv = MLA_Q_RANK
    o_kpe = o_ckv + MLA_KV_RANK
    o_dq = o_kpe + LANES
    o_dk = o_dq + DQ_W
    o_dv = o_dk + DK_W

    cqn = _rms(proj[:, :MLA_Q_RANK], gqa_ref[...])
    q = _dot(cqn.astype(BF16), wuq_ref[...])
    qn = _block_norm(q, bdq_ref[...], gq_ref[...])
    q_first = (lane >= MLA_NOPE) & (lane < MLA_NOPE + MLA_ROPE // 2)
    qf = _rope(qn, cosq_ref[...], sinq_ref[...], MLA_ROPE // 2, q_first)
    qf_ref[...] = (qf * (MLA_SCALE * LOG2E)).astype(BF16)

    ckv = _rms(proj[:, o_ckv:o_kpe], gkva_ref[...])
    ckv_ref[...] = ckv
    ckvb = ckv.astype(BF16)
    ckvt_ref[...] = ckv.T.astype(BF16)
    kr = proj[:, o_kpe:o_dq]
    kn = kr * lax.rsqrt(jnp.sum(kr * kr, axis=-1, keepdims=True) * (1.0 / MLA_ROPE) + EPS) * gkpe_ref[...]
    kpe = _rope(kn, cosk_ref[...], sink_ref[...], MLA_ROPE // 2, lane < MLA_ROPE // 2)
    kpe_ref[...] = kpe[:, :MLA_ROPE]

    kraw = _dot(ckvb, wuk_ref[...])
    knn = _block_norm(kraw, bdq_ref[...], gk_ref[...])
    kpe_at_rope = pltpu.roll(kpe, MLA_NOPE, 1)
    kf_ref[...] = jnp.concatenate(
        [knn[:, i:i + LANES] + kpe_at_rope for i in range(0, QK_W, LANES)], axis=1).astype(BF16)

    d_first = (lane & (DIFF_HD - 1)) < DIFF_HD // 2
    dq = _block_norm(proj[:, o_dq:o_dk], bdd_ref[...], gdq_ref[...])
    dq = _rope(dq, cosd_ref[...], sind_ref[...], DIFF_HD // 2, d_first) * (DIFF_SCALE * LOG2E)
    map0 = (lax.broadcasted_iota(jnp.int32, (1, DQ_W), 1) & (LANES - 1)) < DIFF_HD
    dq0_ref[...] = jnp.where(map0, dq, 0.0).astype(BF16)
    dq1_ref[...] = jnp.where(map0, 0.0, dq).astype(BF16)
    dk = _block_norm(proj[:, o_dk:o_dv], bdd_ref[...], gdk_ref[...])
    dk = _rope(dk, cosd_ref[...], sind_ref[...], DIFF_HD // 2, d_first)
    dk_ref[...] = dk
    dkb_ref[...] = dk.astype(BF16)
    dv = proj[:, o_dv:]
    dv_ref[...] = dv
    dvt_ref[...] = dv.T.astype(BF16)


def _proj(x3, sc, sh, tables, wts, tm):
    b, t, d = x3.shape
    nt = t // tm
    per_tok = sc.shape[1] != 1
    tab_rows = tables[0].shape[0]

    def tok_spec(w):
        return pl.BlockSpec((None, tm, w), lambda s, bb: (bb, s, 0))

    mod_spec = tok_spec(d) if per_tok else pl.BlockSpec((None, 1, d), lambda s, bb: (bb, 0, 0))
    tab_spec = (pl.BlockSpec((tm, LANES), lambda s, bb: (s, 0)) if tab_rows != 1
                else pl.BlockSpec((1, LANES), lambda s, bb: (0, 0)))
    in_specs = [tok_spec(d), mod_spec, mod_spec] + [tab_spec] * 6 + [_const_spec(w.shape) for w in wts]
    outs = [(QK_W, BF16, False), (QK_W, BF16, False), (DQ_W, BF16, False), (DQ_W, BF16, False),
            (MLA_KV_RANK, BF16, True), (DK_W, BF16, False), (DV_W, BF16, True),
            (MLA_KV_RANK, F32, False), (MLA_ROPE, F32, False), (DK_W, F32, False), (DV_W, F32, False)]

    def out_spec(w, tr):
        return pl.BlockSpec((None, w, tm), lambda s, bb: (bb, 0, s)) if tr else tok_spec(w)

    return pl.pallas_call(
        _proj_kernel,
        grid=(nt, b),
        in_specs=in_specs,
        out_specs=[out_spec(w, tr) for w, _, tr in outs],
        out_shape=[jax.ShapeDtypeStruct((b, w, t) if tr else (b, t, w), dt) for w, dt, tr in outs],
        compiler_params=_cparams(("parallel", "parallel")),
        name="proj",
    )(x3, sc, sh, *tables, *wts)


def _lam(lq1_ref, lk1_ref, lq2_ref, lk2_ref, lam_init):
    a = jnp.sum(lq1_ref[...] * lk1_ref[...], axis=-1, keepdims=True)
    b = jnp.sum(lq2_ref[...] * lk2_ref[...], axis=-1, keepdims=True)
    return jnp.exp(a) - jnp.exp(b) + lam_init


ATTN_MAPS = MLA_HEADS + 2 * DIFF_HEADS
ATTN_SLOTS = 2


def _col_tree(x, op):
    parts = [x[c * 64:(c + 1) * 64] for c in range(x.shape[0] // 64)]
    while len(parts) > 1:
        parts = [op(parts[a], parts[a + 1]) for a in range(0, len(parts), 2)]
    return parts[0]


def _attn_kernel(qf_ref, dq0_ref, dq1_ref, kf_ref, ckvt_ref, dk_ref, dvt_ref, wuvt_ref, gsub_ref,
                 lq1_ref, lk1_ref, lq2_ref, lk2_ref, out_ref, m_s, l_s, acc_s, s_scr, *, tq, lam_init):
    i = pl.program_id(1)
    j = pl.program_id(2)

    @pl.when(j == 0)
    def _():
        m_s[...] = jnp.full(m_s.shape, NEG, F32)
        l_s[...] = jnp.zeros(l_s.shape, F32)
        acc_s[...] = jnp.zeros(acc_s.shape, F32)

    maps = []
    for h in range(MLA_HEADS):
        sl = slice(h * LANES, (h + 1) * LANES)
        maps.append((qf_ref, sl, kf_ref, sl, None, h))
    for g in range(DIFF_KV_HEADS):
        gs = slice(g * LANES, (g + 1) * LANES)
        for r in range(DIFF_HEADS // DIFF_KV_HEADS):
            sl = slice((g * 2 + r) * LANES, (g * 2 + r + 1) * LANES)
            maps.append((dq0_ref, sl, dk_ref, gs, gs, MLA_HEADS + (g * 2 + r) * 2))
            maps.append((dq1_ref, sl, dk_ref, gs, gs, MLA_HEADS + (g * 2 + r) * 2 + 1))

    def scores(n, keep):
        q_ref, qs, k_ref, ks, _, _ = maps[n]
        st = _dot_nt(k_ref[:, ks], q_ref[:, qs])
        if keep is not None:
            st = jnp.where(keep, st, NEG)
        s_scr[n % ATTN_SLOTS] = st

    def softmax_pv(n):
        _, _, _, _, vs, idx = maps[n]
        vt = ckvt_ref[...] if vs is None else dvt_ref[vs, :]
        st = s_scr[n % ATTN_SLOTS]
        m_prev = m_s[idx]
        m_new = jnp.maximum(m_prev, jnp.max(_col_tree(st, jnp.maximum), axis=0, keepdims=True))
        alpha = jnp.exp2(m_prev - m_new)
        p = jnp.exp2(st - m_new)
        l_s[idx] = alpha * l_s[idx] + jnp.sum(_col_tree(p, jnp.add), axis=0, keepdims=True)
        acc_s[idx] = alpha * acc_s[idx] + _dot(vt, p.astype(BF16))
        m_s[idx] = m_new

    def step(masked):
        keep = None
        if masked:
            keep = lax.broadcasted_iota(jnp.int32, (tq, tq), 0) <= lax.broadcasted_iota(jnp.int32, (tq, tq), 1)
        scores(0, keep)
        for n in range(ATTN_MAPS):
            if n + 1 < ATTN_MAPS:
                scores(n + 1, keep)
            softmax_pv(n)

    @pl.when(j < i)
    def _():
        step(False)

    @pl.when(j == i)
    def _():
        step(True)
        outs = []
        for h in range(MLA_HEADS):
            lat_t = (acc_s[h] / l_s[h]).astype(BF16)
            outs.append(_dot(wuvt_ref[h], lat_t))
        lam = _lam(lq1_ref, lk1_ref, lq2_ref, lk2_ref, lam_init)
        for gr in range(DIFF_HEADS):
            i0 = MLA_HEADS + 2 * gr
            d = acc_s[i0] / l_s[i0] - lam * (acc_s[i0 + 1] / l_s[i0 + 1])
            d = d * lax.rsqrt(jnp.mean(d * d, axis=0, keepdims=True) + EPS) * gsub_ref[...]
            outs.append(d * (1.0 - lam_init))
        out_ref[...] = jnp.concatenate(outs, axis=0).T.astype(out_ref.dtype)


def _attn(qf, dq0, dq1, kf, ckvt, dkb, dvt, wuvt, gsub_col, lams, lam_init, tq):
    b, s, _ = qf.shape
    nq = s // tq
    assert s % tq == 0 and tq % 64 == 0

    def q_spec(w):
        return pl.BlockSpec((None, tq, w), lambda bb, i, j: (bb, i, 0))

    def k_spec(w):
        return pl.BlockSpec((None, tq, w), lambda bb, i, j: (bb, jnp.minimum(i, j), 0))

    def kt_spec(w):
        return pl.BlockSpec((None, w, tq), lambda bb, i, j: (bb, 0, jnp.minimum(i, j)))

    mix_w = MLA_HEADS * MLA_V + DIFF_HEADS * DIFF_VD
    return pl.pallas_call(
        functools.partial(_attn_kernel, tq=tq, lam_init=lam_init),
        grid=(b, nq, nq),
        in_specs=[q_spec(QK_W), q_spec(DQ_W), q_spec(DQ_W), k_spec(QK_W), kt_spec(MLA_KV_RANK), k_spec(DK_W),
                  kt_spec(DV_W), _const_spec(wuvt.shape), _const_spec(gsub_col.shape)]
                 + [_const_spec(l.shape) for l in lams],
        out_specs=q_spec(mix_w),
        out_shape=jax.ShapeDtypeStruct((b, s, mix_w), BF16),
        scratch_shapes=[pltpu.VMEM((ATTN_MAPS, 1, tq), F32), pltpu.VMEM((ATTN_MAPS, 1, tq), F32),
                        pltpu.VMEM((ATTN_MAPS, DIFF_VD, tq), F32), pltpu.VMEM((ATTN_SLOTS, tq, tq), F32)],
        compiler_params=_cparams(("parallel", "parallel", "arbitrary")),
        name="attn",
    )(qf, dq0, dq1, kf, ckvt, dkb, dvt, wuvt, gsub_col, *lams)


def _qabs_kernel(qf_ref, gk_ref, wt_ref, qa_ref):
    for h in range(MLA_HEADS):
        sl = slice(h * LANES, (h + 1) * LANES)
        qg = (qf_ref[:, sl].astype(F32) * gk_ref[...]).astype(BF16)
        qa_ref[:, sl] = _dot(qg, wt_ref[h])


def _qabs(qf, gk_row, wuk_t):
    n = qf.shape[0]
    return pl.pallas_call(
        _qabs_kernel,
        grid=(1,),
        in_specs=[_const_spec(qf.shape), _const_spec(gk_row.shape), _const_spec(wuk_t.shape)],
        out_specs=_const_spec((n, QK_W)),
        out_shape=jax.ShapeDtypeStruct((n, QK_W), F32),
        compiler_params=_cparams(("arbitrary",)),
        name="qabs",
    )(qf, gk_row, wuk_t)


def _dec_pipeline(refs, *, layer, n_seq, n_pages, lam_init):
    (pt_ref, qa_ref, qp_ref, qbd_ref, qf8_ref, kf8_ref, dkrow_ref, ckvrow_ref, dvrow_ref,
     wukt_ref, wuv_ref, gsub_ref, lq1_ref, lk1_ref, lq2_ref, lk2_ref,
     ckv_hbm, kpe_hbm, kt_hbm, v_hbm, oa_ref, ob_ref,
     ckv_buf, kpe_buf, kt_buf, v_buf, sems, lhs_ref, m_a, l_a, acc_a, m_d, l_d, acc_d) = refs
    pp = DEC_PAGES
    n_steps = n_pages // pp
    total = n_seq * n_steps
    hbm = (ckv_hbm, kpe_hbm, kt_hbm, v_hbm)
    bufs = (ckv_buf, kpe_buf, kt_buf, v_buf)
    nk = MLA_HEADS * MLA_NOPE
    n_stages = DEC_CHAINS + DEC_SKEW

    def page_copy(a, slot, k, page):
        return pltpu.make_async_copy(hbm[a].at[layer, page], bufs[a].at[slot, k], sems.at[a, slot])

    def start_step(t, slot):
        b = t // n_steps
        first = (t - b * n_steps) * pp
        for k in range(pp):
            page = pt_ref[b, first + k]
            for a in range(len(hbm)):
                page_copy(a, slot, k, page).start()

    def wait_step(slot):
        for k in range(pp):
            for a in range(len(hbm)):
                page_copy(a, slot, k, 0).wait()

    def update(s, m_ref, l_ref, c):
        m_prev = m_ref[c]
        m_new = jnp.maximum(m_prev, jnp.max(s, axis=-1, keepdims=True))
        alpha = jnp.exp2(m_prev - m_new)
        p = jnp.exp2(s - m_new)
        l_ref[c] = alpha * l_ref[c] + jnp.sum(p, axis=-1, keepdims=True)
        m_ref[c] = m_new
        return alpha, p

    def pages(c):
        return range(c * DEC_CHAIN_PAGES, (c + 1) * DEC_CHAIN_PAGES)

    def stages(t):
        slot = t & 1
        b = t // n_steps
        j = t - b * n_steps
        row = lax.broadcasted_iota(jnp.int32, (MLA_HEADS, 1), 0)
        row_g0 = ((row >> 1) & 1) == 0
        prods = {}

        def begin():
            start_step(jnp.minimum(t + 1, total - 1), 1 - slot)
            wait_step(slot)
            lhs_ref[nk:, :] = jnp.concatenate(
                [qa_ref[b].astype(BF16), jnp.zeros((lhs_ref.shape[0] - nk - MLA_HEADS, LANES), BF16)], axis=0)
            for r in (m_a, m_d):
                r[...] = jnp.where(j == 0, NEG, r[...])
            for r in (l_a, acc_a, l_d, acc_d):
                r[...] = jnp.where(j == 0, 0.0, r[...])

        def score_products(c):
            ckv = jnp.concatenate([ckv_buf[slot, k] for k in pages(c)], axis=0).astype(BF16)
            res = _dot_nt(lhs_ref[...], ckv)
            kpe_t = jnp.concatenate([kpe_buf[slot, k] for k in pages(c)], axis=1).astype(BF16)
            bp = _dot(qp_ref[b], kpe_t)
            kt = jnp.concatenate([kt_buf[slot, k] for k in pages(c)], axis=1).astype(BF16)
            return ckv, res, bp, _dot(qbd_ref[b], kt)

        def softmax(c, res, bp, sd):
            sq = res[:nk] * res[:nk]
            ssq = jnp.concatenate(
                [jnp.sum(sq[h * MLA_NOPE:(h + 1) * MLA_NOPE], axis=0, keepdims=True) for h in range(MLA_HEADS)],
                axis=0)
            rnorm = lax.rsqrt(ssq * (1.0 / MLA_NOPE) + EPS)
            alpha_a, p_a = update(res[nk:nk + MLA_HEADS] * rnorm + bp, m_a, l_a, c)
            alpha_d, p_d = update(sd, m_d, l_d, c)
            return alpha_a, p_a.astype(BF16), alpha_d, p_d.astype(BF16)

        def value_products(c, ckv, alpha_a, p_a, alpha_d, p_d):
            acc_a[c] = alpha_a * acc_a[c] + _dot(p_a, ckv)
            pv = []
            for g in range(DIFF_KV_HEADS):
                v = jnp.concatenate(
                    [v_buf[slot, k, pl.ds(g, PAGE_SIZE, stride=DIFF_KV_HEADS), :] for k in pages(c)], axis=0)
                pv.append(_dot(p_d, v.astype(BF16)))
            acc_d[c] = alpha_d * acc_d[c] + jnp.where(row_g0, pv[0], pv[1])

        def finish():
            def merged(s_self, v_self, m_ref, l_ref, acc_ref):
                m = s_self
                for c in range(DEC_CHAINS):
                    m = jnp.maximum(m, m_ref[c])
                p_self = jnp.exp2(s_self - m)
                l = p_self
                acc = p_self * v_self
                for c in range(DEC_CHAINS):
                    w = jnp.exp2(m_ref[c] - m)
                    l = l + w * l_ref[c]
                    acc = acc + w * acc_ref[c]
                return acc / l

            s_self = jnp.sum(qf8_ref[b].astype(F32) * kf8_ref[b].astype(F32), axis=-1, keepdims=True)
            lat = merged(s_self, ckvrow_ref[b], m_a, l_a, acc_a)
            full = _dot(lat.astype(BF16), wuv_ref[...])
            col_head = lax.broadcasted_iota(jnp.int32, full.shape, 1) >> 6
            row_head = lax.broadcasted_iota(jnp.int32, full.shape, 0)
            oa_ref[b] = jnp.sum(jnp.where(col_head == row_head, full, 0.0), axis=0, keepdims=True)
            s_self = jnp.sum(qbd_ref[b].astype(F32) * dkrow_ref[b].astype(F32), axis=-1, keepdims=True)
            dvrow = dvrow_ref[b]
            v_self = jnp.where(row_g0, dvrow[:, :DIFF_VD], dvrow[:, DIFF_VD:])
            o = merged(s_self, v_self, m_d, l_d, acc_d)
            lam = _lam(lq1_ref, lk1_ref, lq2_ref, lk2_ref, lam_init)
            d = o[:DIFF_HEADS] - lam * o[DIFF_HEADS:]
            ob_ref[b] = _rms(d, gsub_ref[...]) * (1.0 - lam_init)

        def make(step):
            def run():
                if step == 0:
                    begin()
                if step < DEC_CHAINS:
                    prods[step] = score_products(step)
                c = step - DEC_SKEW
                if c >= 0:
                    ckv, res, bp, sd = prods.pop(c)
                    value_products(c, ckv, *softmax(c, res, bp, sd))
                if step == n_stages - 1:
                    finish()
            return run

        return [make(step) for step in range(n_stages)]

    def prime():
        lhs_ref[:nk, :] = wukt_ref[...]
        start_step(0, 0)

    def drain():
        wait_step(total & 1)

    return prime, drain, stages, total


def _dec_scratch(caches):
    return ([pltpu.VMEM((2, DEC_PAGES) + c.shape[2:], c.dtype) for c in caches]
            + [pltpu.SemaphoreType.DMA((len(caches), 2)),
               pltpu.VMEM((MLA_HEADS * MLA_NOPE + 16, LANES), BF16),
               pltpu.VMEM((DEC_CHAINS, MLA_HEADS, 1), F32), pltpu.VMEM((DEC_CHAINS, MLA_HEADS, 1), F32),
               pltpu.VMEM((DEC_CHAINS, MLA_HEADS, MLA_KV_RANK), F32),
               pltpu.VMEM((DEC_CHAINS, MLA_HEADS, 1), F32), pltpu.VMEM((DEC_CHAINS, MLA_HEADS, 1), F32),
               pltpu.VMEM((DEC_CHAINS, MLA_HEADS, DIFF_VD), F32)])


def _dec_kernel(*refs, layer, n_seq, n_pages, lam_init):
    prime, drain, stages, total = _dec_pipeline(
        refs, layer=layer, n_seq=n_seq, n_pages=n_pages, lam_init=lam_init)

    def body(t, carry):
        for run in stages(t):
            run()
        return carry

    prime()
    lax.fori_loop(0, total, body, 0)
    drain()


def _decode(page_table, per_seq, consts, caches, layer, lam_init):
    b, n_pages = page_table.shape

    def full_spec(a):
        return pl.BlockSpec(a.shape, lambda i, pt: (0,) * a.ndim)

    oa_w = MLA_HEADS * MLA_V
    out_shape = [jax.ShapeDtypeStruct((b, 1, oa_w), F32), jax.ShapeDtypeStruct((b, DIFF_HEADS, DIFF_VD), F32)]
    grid_spec = pltpu.PrefetchScalarGridSpec(
        num_scalar_prefetch=1,
        grid=(1,),
        in_specs=[full_spec(a) for a in list(per_seq) + list(consts)]
                 + [pl.BlockSpec(memory_space=pl.ANY)] * len(caches),
        out_specs=[full_spec(o) for o in out_shape],
        scratch_shapes=_dec_scratch(caches),
    )
    return pl.pallas_call(
        functools.partial(_dec_kernel, layer=layer, n_seq=b, n_pages=n_pages, lam_init=lam_init),
        grid_spec=grid_spec,
        out_shape=out_shape,
        compiler_params=_cparams(("arbitrary",)),
        name="decode",
    )(page_table, *per_seq, *consts, *caches)


def _out_kernel(mix_ref, x_ref, gt_ref, sc_ref, sh_ref, wo_ref, g2_ref, wrh_ref, wrl_ref, br_ref,
                x1_ref, h2_ref, gate_ref):
    o = _dot(mix_ref[...], wo_ref[...])
    x1 = x_ref[...] + gt_ref[...] * o
    x1_ref[...] = x1
    h2 = _rms(x1, g2_ref[...]) * (1.0 + sc_ref[...]) + sh_ref[...]
    h2_ref[...] = h2.astype(BF16)
    hh, hl = _split(h2)
    logits = _dot(hh, wrh_ref[...]) + _dot(hh, wrl_ref[...]) + _dot(hl, wrh_ref[...]) + br_ref[...]
    lane_i = lax.broadcasted_iota(jnp.int32, logits.shape, 1)
    lane = lane_i.astype(F32)
    big = float(ROUTER_LANES)
    gl = jnp.where(lane_i < N_GROUPS, logits, NEG)
    gmax = jnp.max(gl, axis=-1, keepdims=True)
    gidx = jnp.min(jnp.where(gl == gmax, lane, big), axis=-1, keepdims=True)
    g_w = 1.0 / jnp.sum(jnp.exp(gl - gmax), axis=-1, keepdims=True)
    in_group = (lane_i >= N_GROUPS) & (lane_i < N_GROUPS + N_ROUTED) & (
        ((lane_i - N_GROUPS) >> 3).astype(F32) == gidx)
    el = jnp.where(in_group, logits, NEG)
    e1 = jnp.max(el, axis=-1, keepdims=True)
    i1 = jnp.min(jnp.where(el == e1, lane, big), axis=-1, keepdims=True)
    el2 = jnp.where(lane == i1, NEG, el)
    e2 = jnp.max(el2, axis=-1, keepdims=True)
    i2 = jnp.min(jnp.where(el2 == e2, lane, big), axis=-1, keepdims=True)
    t = jnp.exp(e2 - e1)
    w1 = 1.0 / (1.0 + t)
    w2 = t / (1.0 + t)
    gate_ref[...] = jnp.where(lane == i1, w1, jnp.where(lane == i2, w2, 0.0)) * g_w


def _out(mix, x3, gt, sc, sh, wo, g2, wrh, wrl, br, tm):
    b, t, d = x3.shape
    per_tok = gt.shape[1] != 1

    def tok_spec(w):
        return pl.BlockSpec((None, tm, w), lambda s, bb: (bb, s, 0))

    mod_spec = tok_spec(d) if per_tok else pl.BlockSpec((None, 1, d), lambda s, bb: (bb, 0, 0))
    consts = [wo, g2, wrh, wrl, br]
    return pl.pallas_call(
        _out_kernel,
        grid=(t // tm, b),
        in_specs=[tok_spec(mix.shape[2]), tok_spec(d), mod_spec, mod_spec, mod_spec] + [_const_spec(c.shape) for c in consts],
        out_specs=[tok_spec(d), tok_spec(d), tok_spec(ROUTER_LANES)],
        out_shape=[jax.ShapeDtypeStruct((b, t, d), F32), jax.ShapeDtypeStruct((b, t, d), BF16),
                   jax.ShapeDtypeStruct((b, t, ROUTER_LANES), F32)],
        compiler_params=_cparams(("parallel", "parallel")),
        name="out",
    )(mix, x3, gt, sc, sh, *consts)


MOE_CHUNK = MXU_DIM


def _moe_kernel(h2_ref, gate_ref, x1_ref, gt_ref, wg_ref, wu_ref, wd_ref, ex_ref, y_ref):
    g = pl.program_id(2)

    @pl.when(g == 0)
    def _():
        y_ref[...] = jnp.zeros(y_ref.shape, F32)

    h = h2_ref[...]
    gate_b = gate_ref[...].astype(BF16)
    for c in range(GROUP_FF // MOE_CHUNK):
        cs = slice(c * MOE_CHUNK, (c + 1) * MOE_CHUNK)
        ge = _dot(gate_b, ex_ref[:, cs])
        a = _silu(_dot(h, wg_ref[:, cs])) * _dot(h, wu_ref[:, cs]) * ge
        y_ref[...] += _dot(a.astype(BF16), wd_ref[cs, :])

    @pl.when(g == N_GROUPS - 1)
    def _():
        y_ref[...] = x1_ref[...] + gt_ref[...] * y_ref[...]


def _moe(h2, gate, x1, gt, wg, wu, wd, ex, tm):
    b, t, d = x1.shape
    per_tok = gt.shape[1] != 1

    def tok_spec(w):
        return pl.BlockSpec((None, tm, w), lambda s, bb, g: (bb, s, 0))

    mod_spec = tok_spec(d) if per_tok else pl.BlockSpec((None, 1, d), lambda s, bb, g: (bb, 0, 0))

    def grp_spec(a):
        return pl.BlockSpec((None,) + a.shape[1:], lambda s, bb, g: (g, 0, 0))

    return pl.pallas_call(
        _moe_kernel,
        grid=(t // tm, b, N_GROUPS),
        in_specs=[tok_spec(d), tok_spec(ROUTER_LANES), tok_spec(d), mod_spec,
                  grp_spec(wg), grp_spec(wu), grp_spec(wd), grp_spec(ex)],
        out_specs=tok_spec(d),
        out_shape=jax.ShapeDtypeStruct((b, t, d), F32),
        compiler_params=_cparams(("parallel", "parallel", "arbitrary")),
        name="moe",
    )(h2, gate, x1, gt, wg, wu, wd, ex)


def _token_tiles(seq):
    return min(512, seq), min(512, seq), min(512, seq), min(1024, seq)


def _rope_tables(pos):
    def cs(dim):
        half = dim // 2
        inv = ROPE_THETA ** (-jnp.arange(half, dtype=F32) * 2.0 / dim)
        ang = pos[:, None] * inv[None, :]
        c, s = jnp.cos(ang), jnp.sin(ang)
        return jnp.concatenate([c, c], axis=1), jnp.concatenate([-s, s], axis=1)

    t = pos.shape[0]
    c32, s32 = cs(MLA_ROPE)
    c64, s64 = cs(DIFF_HD)
    z = lambda w: jnp.zeros((t, w), F32)
    pad = LANES - MLA_NOPE - MLA_ROPE
    cosq = jnp.concatenate([jnp.ones((t, MLA_NOPE), F32), c32, z(pad)], axis=1)
    sinq = jnp.concatenate([z(MLA_NOPE), s32, z(pad)], axis=1)
    cosk = jnp.concatenate([c32, z(LANES - MLA_ROPE)], axis=1)
    sink = jnp.concatenate([s32, z(LANES - MLA_ROPE)], axis=1)
    cosd = jnp.concatenate([c64, c64], axis=1)
    sind = jnp.concatenate([s64, s64], axis=1)
    return [cosq, sinq, cosk, sink, cosd, sind]


def _block_diag_mean(sizes, width):
    m = jnp.zeros((width, width), F32)
    o = 0
    while o < width:
        for sz in sizes:
            if sz > 0:
                m = m.at[o:o + sz, o:o + sz].set(1.0 / sz)
            o += abs(sz)
    return m.astype(BF16)


def _layer_weights(l, w_in, g_norm1, g_mla_qa, w_mla_uq, g_mla_kva, w_mla_uk, g_mla_qn_nope, g_mla_qn_rope,
                   g_mla_kn_nope, g_mla_kn_rope, g_diff_qn, g_diff_kn):
    d = w_in.shape[1]
    o_kpe = MLA_Q_RANK + MLA_KV_RANK
    wi = w_in[l]
    win = jnp.concatenate([wi[:, :o_kpe], wi[:, o_kpe:o_kpe + MLA_ROPE], jnp.zeros((d, LANES - MLA_ROPE), F32),
                           wi[:, o_kpe + MLA_ROPE:]], axis=1).astype(BF16)
    pad = LANES - MLA_NOPE - MLA_ROPE
    wuq = w_mla_uq[l].reshape(MLA_Q_RANK, MLA_HEADS, MLA_NOPE + MLA_ROPE)
    wuq = jnp.concatenate([wuq, jnp.zeros((MLA_Q_RANK, MLA_HEADS, pad), F32)], axis=2).reshape(MLA_Q_RANK, QK_W).astype(BF16)
    wuk = jnp.concatenate([w_mla_uk[l], jnp.zeros((MLA_KV_RANK, MLA_HEADS, LANES - MLA_NOPE), F32)], axis=2)
    wuk = wuk.reshape(MLA_KV_RANK, QK_W).astype(BF16)
    gq = jnp.tile(jnp.concatenate([g_mla_qn_nope[l], g_mla_qn_rope[l], jnp.zeros((pad,), F32)]), MLA_HEADS)[None]
    gk = jnp.tile(jnp.concatenate([g_mla_kn_nope[l], jnp.zeros((LANES - MLA_NOPE,), F32)]), MLA_HEADS)[None]
    gkpe = jnp.concatenate([g_mla_kn_rope[l], jnp.zeros((LANES - MLA_ROPE,), F32)])[None]
    gdq = jnp.tile(g_diff_qn[l], DQ_W // DIFF_HD)[None]
    gdk = jnp.tile(g_diff_kn[l], DK_W // DIFF_HD)[None]
    bdq = _block_diag_mean((MLA_NOPE, MLA_ROPE, -pad), MXU_DIM)
    bdd = _block_diag_mean((DIFF_HD,), MXU_DIM)
    return [g_norm1[l][None], win, g_mla_qa[l][None], wuq, gq, g_mla_kva[l][None], wuk, gk, gkpe, gdq, gdk, bdq, bdd]


def kernel(x_prompt, x_sample, cache_mla_ckv, cache_mla_kpe, cache_diff_k, cache_diff_v, page_table, c_prompt, c_sample, w_ada, b_ada, g_norm1, w_in, g_mla_qa, w_mla_uq, g_mla_kva, w_mla_uk, w_mla_uv, g_mla_qn_nope, g_mla_qn_rope, g_mla_kn_nope, g_mla_kn_rope, g_diff_qn, g_diff_kn, lam_q1, lam_k1, lam_q2, lam_k2, g_diff_subln, w_o, g_norm2, w_router_group, b_router_group, w_router_expert, b_router_expert, w_exp_gate, w_exp_up, w_exp_down):
    bp, sp, d = x_prompt.shape
    bs, ts, _ = x_sample.shape
    depth = w_in.shape[0]
    n_pool = cache_mla_ckv.shape[1]
    n_pages = page_table.shape[1]
    assert ts == 1 and n_pages % DEC_PAGES == 0 and cache_mla_ckv.shape[2] == PAGE_SIZE
    assert w_in.shape[2] == PROJ_W - LANES + MLA_ROPE and d % MXU_DIM == 0
    past = n_pages * PAGE_SIZE

    tm_p, tq, tm_o, tm_m = _token_tiles(sp)
    tables_p = _rope_tables(jnp.arange(sp, dtype=F32))
    tables_s = _rope_tables(jnp.arange(ts, dtype=F32) + past)

    kpe_t = jnp.transpose(cache_mla_kpe, (0, 1, 3, 2))
    k_t = jnp.transpose(cache_diff_k, (0, 1, 3, 4, 5, 2)).reshape(depth, n_pool, DK_W, PAGE_SIZE)
    v_rows = cache_diff_v.reshape(depth, n_pool, PAGE_SIZE * DIFF_KV_HEADS, DIFF_VD)

    xp = x_prompt
    xs = x_sample.reshape(1, bs, d)
    outs_p = [[], [], [], []]
    outs_s = [[], [], [], []]
    for l in range(depth):
        lam_init = 0.8 - 0.6 * math.exp(-0.3 * l)
        wts = _layer_weights(l, w_in, g_norm1, g_mla_qa, w_mla_uq, g_mla_kva, w_mla_uk, g_mla_qn_nope,
                             g_mla_qn_rope, g_mla_kn_nope, g_mla_kn_rope, g_diff_qn, g_diff_kn)
        lams = [lam_q1[l][None], lam_k1[l][None], lam_q2[l][None], lam_k2[l][None]]
        gsub = g_diff_subln[l][None]
        wuv = w_mla_uv[l]
        wuv_flat = wuv.reshape(MLA_KV_RANK, MLA_HEADS * MLA_V).astype(BF16)
        wuvt = jnp.transpose(wuv, (1, 2, 0)).astype(BF16)
        wuk_t = jnp.concatenate([jnp.transpose(w_mla_uk[l], (1, 2, 0)),
                                 jnp.zeros((MLA_HEADS, LANES - MLA_NOPE, MLA_KV_RANK), F32)], axis=1).astype(BF16)
        wukt_rows = jnp.transpose(w_mla_uk[l], (1, 2, 0)).reshape(MLA_HEADS * MLA_NOPE, MLA_KV_RANK).astype(BF16)
        gk_row = jnp.concatenate([g_mla_kn_nope[l], jnp.zeros((LANES - MLA_NOPE,), F32)])[None]
        wo = w_o[l].astype(BF16)
        wr = jnp.concatenate([w_router_group[l], jnp.transpose(w_router_expert[l], (1, 0, 2)).reshape(d, N_ROUTED),
                              jnp.zeros((d, ROUTER_LANES - N_GROUPS - N_ROUTED), F32)], axis=1)
        wrh, wrl = _split(wr)
        br = jnp.concatenate([b_router_group[l], b_router_expert[l].reshape(N_ROUTED),
                              jnp.zeros((ROUTER_LANES - N_GROUPS - N_ROUTED,), F32)])[None]
        wg = jnp.transpose(w_exp_gate[l], (0, 2, 1, 3)).reshape(N_GROUPS, d, GROUP_FF).astype(BF16)
        wu = jnp.transpose(w_exp_up[l], (0, 2, 1, 3)).reshape(N_GROUPS, d, GROUP_FF).astype(BF16)
        wd = w_exp_down[l].reshape(N_GROUPS, GROUP_FF, d).astype(BF16)
        lane = jnp.arange(ROUTER_LANES)[None, :, None]
        col = jnp.arange(GROUP_FF)[None, None, :]
        grp = jnp.arange(N_GROUPS)[:, None, None]
        ex = (lane == N_GROUPS + grp * EXPERTS_PER_GROUP + col // EXPERT_FF).astype(BF16)

        mod = _ada(jnp.concatenate([c_prompt, c_sample], axis=0), w_ada[l], b_ada[l][None])
        mod_p = mod[:bp].reshape(bp, 6, 1, d)
        mod_s = mod[bp:].reshape(1, bs, 6, d)
        sh1p, sc1p, gt1p, sh2p, sc2p, gt2p = [mod_p[:, k] for k in range(6)]
        sh1s, sc1s, gt1s, sh2s, sc2s, gt2s = [mod_s[:, :, k] for k in range(6)]

        (qf, kf, dq0, dq1, ckvt, dkb, dvt, ckv, kpe, dk, dv) = _proj(xp, sc1p, sh1p, tables_p, wts, tm_p)
        mix = _attn(qf, dq0, dq1, kf, ckvt, dkb, dvt, wuvt, gsub.reshape(DIFF_VD, 1), lams, lam_init, tq)
        x1, h2, gate = _out(mix, xp, gt1p, sc2p, sh2p, wo, g_norm2[l][None], wrh, wrl, br, tm_o)
        for lst, a in zip(outs_p, (ckv, kpe, dk, dv)):
            lst.append(a)
        (qf, kf, dq0, dq1, _, dkb, _, ckv, kpe, dk, dv) = _proj(xs, sc1s, sh1s, tables_s, wts, bs)
        for lst, a in zip(outs_s, (ckv, kpe, dk, dv)):
            lst.append(a)
        qf2 = qf.reshape(bs, QK_W)
        qa = _qabs(qf2, gk_row, wuk_t).reshape(bs, MLA_HEADS, LANES)
        qf8 = qf2.reshape(bs, MLA_HEADS, LANES)
        kf8 = kf.reshape(bs, MLA_HEADS, LANES)
        qp = qf8[:, :, MLA_NOPE:MLA_NOPE + MLA_ROPE]
        dq5 = (dq0 + dq1).reshape(bs, DIFF_KV_HEADS, 2, 2, DIFF_HD)
        eye = jnp.eye(2, dtype=BF16)
        qbd = jnp.einsum('bgrmd,gh,mn->bmgrhnd', dq5, eye, eye).reshape(bs, 2 * DIFF_HEADS, DK_W)
        per_seq = [qa, qp, qbd, qf8, kf8, dkb.reshape(bs, 1, DK_W), ckv.reshape(bs, 1, MLA_KV_RANK),
                   dv.reshape(bs, 1, DV_W)]
        consts = [wukt_rows, wuv_flat, gsub] + lams
        caches = (cache_mla_ckv, kpe_t, k_t, v_rows)

        xp = _moe(h2, gate, x1, gt2p, wg, wu, wd, ex, tm_m)
        oa, ob = _decode(page_table, per_seq, consts, caches, l, lam_init)
        mix = jnp.concatenate([oa.reshape(bs, -1), ob.reshape(bs, -1)], axis=1).astype(BF16).reshape(1, bs, -1)
        x1, h2, gate = _out(mix, xs, gt1s, sc2s, sh2s, wo, g_norm2[l][None], wrh, wrl, br, bs)
        xs = _moe(h2, gate, x1, gt2s, wg, wu, wd, ex, bs)

    def stack_p(lst, tail):
        return jnp.stack(lst).reshape((depth, bp, sp) + tail)

    def stack_s(lst, tail):
        return jnp.stack(lst).reshape((depth, bs, ts) + tail)

    k_tail = (DIFF_KV_HEADS, 2, DIFF_HD)
    v_tail = (DIFF_KV_HEADS, DIFF_VD)
    return (xp, xs.reshape(bs, ts, d),
            stack_p(outs_p[0], (MLA_KV_RANK,)), stack_p(outs_p[1], (MLA_ROPE,)), stack_p(outs_p[2], k_tail),
            stack_p(outs_p[3], v_tail),
            stack_s(outs_s[0], (MLA_KV_RANK,)), stack_s(outs_s[1], (MLA_ROPE,)), stack_s(outs_s[2], k_tail),
            stack_s(outs_s[3], v_tail))
```

```python
import functools
import math

import jax
import jax.numpy as jnp
from jax import lax
from jax.experimental import pallas as pl
from jax.experimental.pallas import tpu as pltpu

F32 = jnp.float32
BF16 = jnp.bfloat16

MLA_HEADS = 8
MLA_Q_RANK = 256
MLA_KV_RANK = 128
MLA_NOPE = 64
MLA_ROPE = 32
MLA_V = 64
DIFF_HEADS = 4
DIFF_KV_HEADS = 2
DIFF_HD = 64
DIFF_VD = 128
N_GROUPS = 4
EXPERTS_PER_GROUP = 8
EXPERT_FF = 128
PAGE_SIZE = 128
ROPE_THETA = 10000.0
EPS = 1e-6
LOG2E = 1.4426950408889634
MLA_SCALE = (MLA_NOPE + MLA_ROPE) ** -0.5
DIFF_SCALE = DIFF_HD ** -0.5
NEG = -1e30

LANES = 128
MXU_DIM = 256
VMEM_LIMIT = 56 * 1024 * 1024
ROUTER_LANES = 128
N_ROUTED = N_GROUPS * EXPERTS_PER_GROUP
GROUP_FF = EXPERTS_PER_GROUP * EXPERT_FF
QK_W = MLA_HEADS * LANES
DQ_W = DIFF_HEADS * 2 * DIFF_HD
DK_W = DIFF_KV_HEADS * 2 * DIFF_HD
DV_W = DIFF_KV_HEADS * DIFF_VD
PROJ_W = MLA_Q_RANK + MLA_KV_RANK + LANES + DQ_W + DK_W + DV_W
DEC_PAGES = 16
DEC_CHAIN_PAGES = 2
DEC_CHAINS = DEC_PAGES // DEC_CHAIN_PAGES
DEC_SKEW = 2


def _dot(a, b):
    return jnp.dot(a, b, preferred_element_type=F32)


def _dot_nt(a, b):
    return lax.dot_general(a, b, (((1,), (1,)), ((), ())), preferred_element_type=F32)


def _split(a):
    hi = a.astype(BF16)
    lo = (a - hi.astype(F32)).astype(BF16)
    return hi, lo


def _rms(v, g):
    return v * lax.rsqrt(jnp.mean(v * v, axis=-1, keepdims=True) + EPS) * g


def _silu(v):
    return v / (1.0 + jnp.exp(-v))


def _cparams(sem):
    return pltpu.CompilerParams(dimension_semantics=sem, vmem_limit_bytes=VMEM_LIMIT)


def _const_spec(shape):
    nd = len(shape)
    return pl.BlockSpec(shape, lambda *_: (0,) * nd)


def _ada_kernel(c_ref, w_ref, b_ref, o_ref):
    s = _silu(c_ref[...])
    sh, sl = _split(s)
    wh, wl = _split(w_ref[...])
    o_ref[...] = _dot(sh, wh) + _dot(sh, wl) + _dot(sl, wh) + b_ref[...]


def _ada(c, w, b):
    m, d = c.shape
    n = w.shape[1]
    tn = 512
    return pl.pallas_call(
        _ada_kernel,
        grid=(n // tn,),
        in_specs=[_const_spec((m, d)), pl.BlockSpec((d, tn), lambda i: (0, i)), pl.BlockSpec((1, tn), lambda i: (0, i))],
        out_specs=pl.BlockSpec((m, tn), lambda i: (0, i)),
        out_shape=jax.ShapeDtypeStruct((m, n), F32),
        compiler_params=_cparams(("parallel",)),
        name="ada",
    )(c, w, b)


def _block_norm(v, bd, g):
    w = v.shape[1]
    sq = (v * v).astype(BF16)
    ms = jnp.concatenate([_dot(sq[:, i:i + MXU_DIM], bd) for i in range(0, w, MXU_DIM)], axis=1)
    return v * lax.rsqrt(ms + EPS) * g


def _rope(v, cos, sin, half, first):
    parts = []
    for i in range(0, v.shape[1], LANES):
        s = v[:, i:i + LANES]
        rot = jnp.where(first, pltpu.roll(s, LANES - half, 1), pltpu.roll(s, half, 1))
        parts.append(s * cos + rot * sin)
    return parts[0] if len(parts) == 1 else jnp.concatenate(parts, axis=1)


def _proj_kernel(x_ref, sc_ref, sh_ref, cosq_ref, sinq_ref, cosk_ref, sink_ref, cosd_ref, sind_ref,
                 g1_ref, win_ref, gqa_ref, wuq_ref, gq_ref, gkva_ref, wuk_ref, gk_ref, gkpe_ref, gdq_ref, gdk_ref,
                 bdq_ref, bdd_ref,
                 qf_ref, kf_ref, dq0_ref, dq1_ref, ckvt_ref, dkb_ref, dvt_ref, ckv_ref, kpe_ref, dk_ref, dv_ref):
    lane = lax.broadcasted_iota(jnp.int32, (1, LANES), 1)
    x = x_ref[...]
    h = _rms(x, g1_ref[...]) * (1.0 + sc_ref[...]) + sh_ref[...]
    proj = _dot(h.astype(BF16), win_ref[...])
    o_ckv = MLA_Q_RANK
    o_kpe = o_ckv + MLA_KV_RANK
    o_dq = o_kpe + LANES
    o_dk = o_dq + DQ_W
    o_dv = o_dk + DK_W

    cqn = _rms(proj[:, :MLA_Q_RANK], gqa_ref[...])
    q = _dot(cqn.astype(BF16), wuq_ref[...])
    qn = _block_norm(q, bdq_ref[...], gq_ref[...])
    q_first = (lane >= MLA_NOPE) & (lane < MLA_NOPE + MLA_ROPE // 2)
    qf = _rope(qn, cosq_ref[...], sinq_ref[...], MLA_ROPE // 2, q_first)
    qf_ref[...] = (qf * (MLA_SCALE * LOG2E)).astype(BF16)

    ckv = _rms(proj[:, o_ckv:o_kpe], gkva_ref[...])
    ckv_ref[...] = ckv
    ckvb = ckv.astype(BF16)
    ckvt_ref[...] = ckv.T.astype(BF16)
    kr = proj[:, o_kpe:o_dq]
    kn = kr * lax.rsqrt(jnp.sum(kr * kr, axis=-1, keepdims=True) * (1.0 / MLA_ROPE) + EPS) * gkpe_ref[...]
    kpe = _rope(kn, cosk_ref[...], sink_ref[...], MLA_ROPE // 2, lane < MLA_ROPE // 2)
    kpe_ref[...] = kpe[:, :MLA_ROPE]

    kraw = _dot(ckvb, wuk_ref[...])
    knn = _block_norm(kraw, bdq_ref[...], gk_ref[...])
    kpe_at_rope = pltpu.roll(kpe, MLA_NOPE, 1)
    kf_ref[...] = jnp.concatenate(
        [knn[:, i:i + LANES] + kpe_at_rope for i in range(0, QK_W, LANES)], axis=1).astype(BF16)

    d_first = (lane & (DIFF_HD - 1)) < DIFF_HD // 2
    dq = _block_norm(proj[:, o_dq:o_dk], bdd_ref[...], gdq_ref[...])
    dq = _rope(dq, cosd_ref[...], sind_ref[...], DIFF_HD // 2, d_first) * (DIFF_SCALE * LOG2E)
    map0 = (lax.broadcasted_iota(jnp.int32, (1, DQ_W), 1) & (LANES - 1)) < DIFF_HD
    dq0_ref[...] = jnp.where(map0, dq, 0.0).astype(BF16)
    dq1_ref[...] = jnp.where(map0, 0.0, dq).astype(BF16)
    dk = _block_norm(proj[:, o_dk:o_dv], bdd_ref[...], gdk_ref[...])
    dk = _rope(dk, cosd_ref[...], sind_ref[...], DIFF_HD // 2, d_first)
    dk_ref[...] = dk
    dkb_ref[...] = dk.astype(BF16)
    dv = proj[:, o_dv:]
    dv_ref[...] = dv
    dvt_ref[...] = dv.T.astype(BF16)


def _proj(x3, sc, sh, tables, wts, tm):
    b, t, d = x3.shape
    nt = t // tm
    per_tok = sc.shape[1] != 1
    tab_rows = tables[0].shape[0]

    def tok_spec(w):
        return pl.BlockSpec((None, tm, w), lambda s, bb: (bb, s, 0))

    mod_spec = tok_spec(d) if per_tok else pl.BlockSpec((None, 1, d), lambda s, bb: (bb, 0, 0))
    tab_spec = (pl.BlockSpec((tm, LANES), lambda s, bb: (s, 0)) if tab_rows != 1
                else pl.BlockSpec((1, LANES), lambda s, bb: (0, 0)))
    in_specs = [tok_spec(d), mod_spec, mod_spec] + [tab_spec] * 6 + [_const_spec(w.shape) for w in wts]
    outs = [(QK_W, BF16, False), (QK_W, BF16, False), (DQ_W, BF16, False), (DQ_W, BF16, False),
            (MLA_KV_RANK, BF16, True), (DK_W, BF16, False), (DV_W, BF16, True),
            (MLA_KV_RANK, F32, False), (MLA_ROPE, F32, False), (DK_W, F32, False), (DV_W, F32, False)]

    def out_spec(w, tr):
        return pl.BlockSpec((None, w, tm), lambda s, bb: (bb, 0, s)) if tr else tok_spec(w)

    return pl.pallas_call(
        _proj_kernel,
        grid=(nt, b),
        in_specs=in_specs,
        out_specs=[out_spec(w, tr) for w, _, tr in outs],
        out_shape=[jax.ShapeDtypeStruct((b, w, t) if tr else (b, t, w), dt) for w, dt, tr in outs],
        compiler_params=_cparams(("parallel", "parallel")),
        name="proj",
    )(x3, sc, sh, *tables, *wts)


def _lam(lq1_ref, lk1_ref, lq2_ref, lk2_ref, lam_init):
    a = jnp.sum(lq1_ref[...] * lk1_ref[...], axis=-1, keepdims=True)
    b = jnp.sum(lq2_ref[...] * lk2_ref[...], axis=-1, keepdims=True)
    return jnp.exp(a) - jnp.exp(b) + lam_init


ATTN_MAPS = MLA_HEADS + 2 * DIFF_HEADS
ATTN_SLOTS = 2


def _col_tree(x, op):
    parts = [x[c * 64:(c + 1) * 64] for c in range(x.shape[0] // 64)]
    while len(parts) > 1:
        parts = [op(parts[a], parts[a + 1]) for a in range(0, len(parts), 2)]
    return parts[0]


def _attn_kernel(qf_ref, dq0_ref, dq1_ref, kf_ref, ckvt_ref, dk_ref, dvt_ref, wuvt_ref, gsub_ref,
                 lq1_ref, lk1_ref, lq2_ref, lk2_ref, out_ref, m_s, l_s, acc_s, s_scr, *, tq, lam_init):
    i = pl.program_id(1)
    j = pl.program_id(2)

    @pl.when(j == 0)
    def _():
        m_s[...] = jnp.full(m_s.shape, NEG, F32)
        l_s[...] = jnp.zeros(l_s.shape, F32)
        acc_s[...] = jnp.zeros(acc_s.shape, F32)

    maps = []
    for h in range(MLA_HEADS):
        sl = slice(h * LANES, (h + 1) * LANES)
        maps.append((qf_ref, sl, kf_ref, sl, None, h))
    for g in range(DIFF_KV_HEADS):
        gs = slice(g * LANES, (g + 1) * LANES)
        for r in range(DIFF_HEADS // DIFF_KV_HEADS):
            sl = slice((g * 2 + r) * LANES, (g * 2 + r + 1) * LANES)
            maps.append((dq0_ref, sl, dk_ref, gs, gs, MLA_HEADS + (g * 2 + r) * 2))
            maps.append((dq1_ref, sl, dk_ref, gs, gs, MLA_HEADS + (g * 2 + r) * 2 + 1))

    def scores(n, pieces, masked):
        q_ref, qs, k_ref, ks, _, _ = maps[n]
        for q0, q1, nk in pieces:
            st = _dot_nt(k_ref[:nk, ks], q_ref[q0:q1, qs])
            if masked:
                key = lax.broadcasted_iota(jnp.int32, st.shape, 0)
                qry = lax.broadcasted_iota(jnp.int32, st.shape, 1) + q0
                st = jnp.where(key <= qry, st, NEG)
            s_scr[n % ATTN_SLOTS, :nk, q0:q1] = st

    def softmax_pv(n, pieces):
        _, _, _, _, vs, idx = maps[n]
        for q0, q1, nk in pieces:
            vt = ckvt_ref[:, :nk] if vs is None else dvt_ref[vs, :nk]
            st = s_scr[n % ATTN_SLOTS, :nk, q0:q1]
            m_prev = m_s[idx, :, q0:q1]
            m_new = jnp.maximum(m_prev, jnp.max(_col_tree(st, jnp.maximum), axis=0, keepdims=True))
            alpha = jnp.exp2(m_prev - m_new)
            p = jnp.exp2(st - m_new)
            l_s[idx, :, q0:q1] = alpha * l_s[idx, :, q0:q1] + jnp.sum(_col_tree(p, jnp.add), axis=0, keepdims=True)
            acc_s[idx, :, q0:q1] = alpha * acc_s[idx, :, q0:q1] + _dot(vt, p.astype(BF16))
            m_s[idx, :, q0:q1] = m_new

    def step(masked):
        half = tq // 2
        pieces = [(0, half, half), (half, tq, tq)] if masked else [(0, tq, tq)]
        scores(0, pieces, masked)
        for n in range(ATTN_MAPS):
            if n + 1 < ATTN_MAPS:
                scores(n + 1, pieces, masked)
            softmax_pv(n, pieces)

    @pl.when(j < i)
    def _():
        step(False)

    @pl.when(j == i)
    def _():
        step(True)
        outs = []
        for h in range(MLA_HEADS):
            lat_t = (acc_s[h] / l_s[h]).astype(BF16)
            outs.append(_dot(wuvt_ref[h], lat_t))
        lam = _lam(lq1_ref, lk1_ref, lq2_ref, lk2_ref, lam_init)
        for gr in range(DIFF_HEADS):
            i0 = MLA_HEADS + 2 * gr
            d = acc_s[i0] / l_s[i0] - lam * (acc_s[i0 + 1] / l_s[i0 + 1])
            d = d * lax.rsqrt(jnp.mean(d * d, axis=0, keepdims=True) + EPS) * gsub_ref[...]
            outs.append(d * (1.0 - lam_init))
        out_ref[...] = jnp.concatenate(outs, axis=0).T.astype(out_ref.dtype)


def _attn(qf, dq0, dq1, kf, ckvt, dkb, dvt, wuvt, gsub_col, lams, lam_init, tq):
    b, s, _ = qf.shape
    nq = s // tq
    assert s % tq == 0 and tq % (2 * LANES) == 0 and (tq // LANES) & (tq // LANES - 1) == 0

    def q_spec(w):
        return pl.BlockSpec((None, tq, w), lambda bb, i, j: (bb, i, 0))

    def k_spec(w):
        return pl.BlockSpec((None, tq, w), lambda bb, i, j: (bb, jnp.minimum(i, j), 0))

    def kt_spec(w):
        return pl.BlockSpec((None, w, tq), lambda bb, i, j: (bb, 0, jnp.minimum(i, j)))

    mix_w = MLA_HEADS * MLA_V + DIFF_HEADS * DIFF_VD
    return pl.pallas_call(
        functools.partial(_attn_kernel, tq=tq, lam_init=lam_init),
        grid=(b, nq, nq),
        in_specs=[q_spec(QK_W), q_spec(DQ_W), q_spec(DQ_W), k_spec(QK_W), kt_spec(MLA_KV_RANK), k_spec(DK_W),
                  kt_spec(DV_W), _const_spec(wuvt.shape), _const_spec(gsub_col.shape)]
                 + [_const_spec(l.shape) for l in lams],
        out_specs=q_spec(mix_w),
        out_shape=jax.ShapeDtypeStruct((b, s, mix_w), BF16),
        scratch_shapes=[pltpu.VMEM((ATTN_MAPS, 1, tq), F32), pltpu.VMEM((ATTN_MAPS, 1, tq), F32),
                        pltpu.VMEM((ATTN_MAPS, DIFF_VD, tq), F32), pltpu.VMEM((ATTN_SLOTS, tq, tq), F32)],
        compiler_params=_cparams(("parallel", "parallel", "arbitrary")),
        name="attn",
    )(qf, dq0, dq1, kf, ckvt, dkb, dvt, wuvt, gsub_col, *lams)


def _qabs_kernel(qf_ref, gk_ref, wt_ref, qa_ref):
    for h in range(MLA_HEADS):
        sl = slice(h * LANES, (h + 1) * LANES)
        qg = (qf_ref[:, sl].astype(F32) * gk_ref[...]).astype(BF16)
        qa_ref[:, sl] = _dot(qg, wt_ref[h])


def _qabs(qf, gk_row, wuk_t):
    n = qf.shape[0]
    return pl.pallas_call(
        _qabs_kernel,
        grid=(1,),
        in_specs=[_const_spec(qf.shape), _const_spec(gk_row.shape), _const_spec(wuk_t.shape)],
        out_specs=_const_spec((n, QK_W)),
        out_shape=jax.ShapeDtypeStruct((n, QK_W), F32),
        compiler_params=_cparams(("arbitrary",)),
        name="qabs",
    )(qf, gk_row, wuk_t)


def _dec_pipeline(refs, *, layer, n_seq, n_pages, lam_init):
    (pt_ref, qa_ref, qp_ref, qbd_ref, qf8_ref, kf8_ref, dkrow_ref, ckvrow_ref, dvrow_ref,
     wukt_ref, wuv_ref, gsub_ref, lq1_ref, lk1_ref, lq2_ref, lk2_ref,
     ckv_hbm, kpe_hbm, kt_hbm, v_hbm, oa_ref, ob_ref,
     ckv_buf, kpe_buf, kt_buf, v_buf, sems, lhs_ref, m_a, l_a, acc_a, m_d, l_d, acc_d) = refs
    pp = DEC_PAGES
    n_steps = n_pages // pp
    total = n_seq * n_steps
    hbm = (ckv_hbm, kpe_hbm, kt_hbm, v_hbm)
    bufs = (ckv_buf, kpe_buf, kt_buf, v_buf)
    nk = MLA_HEADS * MLA_NOPE
    n_stages = DEC_CHAINS + DEC_SKEW

    def page_copy(a, slot, k, page):
        return pltpu.make_async_copy(hbm[a].at[layer, page], bufs[a].at[slot, k], sems.at[a, slot])

    def start_step(t, slot):
        b = t // n_steps
        first = (t - b * n_steps) * pp
        for k in range(pp):
            page = pt_ref[b, first + k]
            for a in range(len(hbm)):
                page_copy(a, slot, k, page).start()

    def wait_step(slot):
        for k in range(pp):
            for a in range(len(hbm)):
                page_copy(a, slot, k, 0).wait()

    def update(s, m_ref, l_ref, c):
        m_prev = m_ref[c]
        m_new = jnp.maximum(m_prev, jnp.max(s, axis=-1, keepdims=True))
        alpha = jnp.exp2(m_prev - m_new)
        p = jnp.exp2(s - m_new)
        l_ref[c] = alpha * l_ref[c] + jnp.sum(p, axis=-1, keepdims=True)
        m_ref[c] = m_new
        return alpha, p

    def pages(c):
        return range(c * DEC_CHAIN_PAGES, (c + 1) * DEC_CHAIN_PAGES)

    def stages(t):
        slot = t & 1
        b = t // n_steps
        j = t - b * n_steps
        row = lax.broadcasted_iota(jnp.int32, (MLA_HEADS, 1), 0)
        row_g0 = ((row >> 1) & 1) == 0
        prods = {}

        def begin():
            start_step(jnp.minimum(t + 1, total - 1), 1 - slot)
            wait_step(slot)

            @pl.when(j == 0)
            def _():
                lhs_ref[nk:, :] = jnp.concatenate(
                    [qa_ref[b].astype(BF16), jnp.zeros((lhs_ref.shape[0] - nk - MLA_HEADS, LANES), BF16)], axis=0)
                for r in (m_a, m_d):
                    r[...] = jnp.full(r.shape, NEG, F32)
                for r in (l_a, acc_a, l_d, acc_d):
                    r[...] = jnp.zeros(r.shape, F32)

        def score_products(c):
            ckv = jnp.concatenate([ckv_buf[slot, k] for k in pages(c)], axis=0).astype(BF16)
            res = _dot_nt(lhs_ref[...], ckv)
            kpe_t = jnp.concatenate([kpe_buf[slot, k] for k in pages(c)], axis=1).astype(BF16)
            bp = _dot(qp_ref[b], kpe_t)
            kt = jnp.concatenate([kt_buf[slot, k] for k in pages(c)], axis=1).astype(BF16)
            return ckv, res, bp, _dot(qbd_ref[b], kt)

        def softmax(c, res, bp, sd):
            sq = res[:nk] * res[:nk]
            ssq = jnp.concatenate(
                [jnp.sum(sq[h * MLA_NOPE:(h + 1) * MLA_NOPE], axis=0, keepdims=True) for h in range(MLA_HEADS)],
                axis=0)
            rnorm = lax.rsqrt(ssq * (1.0 / MLA_NOPE) + EPS)
            alpha_a, p_a = update(res[nk:nk + MLA_HEADS] * rnorm + bp, m_a, l_a, c)
            alpha_d, p_d = update(sd, m_d, l_d, c)
            return alpha_a, p_a.astype(BF16), alpha_d, p_d.astype(BF16)

        def value_products(c, ckv, alpha_a, p_a, alpha_d, p_d):
            acc_a[c] = alpha_a * acc_a[c] + _dot(p_a, ckv)
            pv = []
            for g in range(DIFF_KV_HEADS):
                v = jnp.concatenate(
                    [v_buf[slot, k, pl.ds(g, PAGE_SIZE, stride=DIFF_KV_HEADS), :] for k in pages(c)], axis=0)
                pv.append(_dot(p_d, v.astype(BF16)))
            acc_d[c] = alpha_d * acc_d[c] + jnp.where(row_g0, pv[0], pv[1])

        def merged(s_self, v_self, m_ref, l_ref, acc_ref):
            m = s_self
            for c in range(DEC_CHAINS):
                m = jnp.maximum(m, m_ref[c])
            p_self = jnp.exp2(s_self - m)
            l = p_self
            acc = p_self * v_self
            for c in range(DEC_CHAINS):
                w = jnp.exp2(m_ref[c] - m)
                l = l + w * l_ref[c]
                acc = acc + w * acc_ref[c]
            return acc / l

        def finish():
            @pl.when(j == n_steps - 1)
            def _():
                s_self = jnp.sum(qf8_ref[b].astype(F32) * kf8_ref[b].astype(F32), axis=-1, keepdims=True)
                lat = merged(s_self, ckvrow_ref[b], m_a, l_a, acc_a)
                full = _dot(lat.astype(BF16), wuv_ref[...])
                col_head = lax.broadcasted_iota(jnp.int32, full.shape, 1) >> 6
                row_head = lax.broadcasted_iota(jnp.int32, full.shape, 0)
                oa_ref[b] = jnp.sum(jnp.where(col_head == row_head, full, 0.0), axis=0, keepdims=True)
                s_self = jnp.sum(qbd_ref[b].astype(F32) * dkrow_ref[b].astype(F32), axis=-1, keepdims=True)
                dvrow = dvrow_ref[b]
                v_self = jnp.where(row_g0, dvrow[:, :DIFF_VD], dvrow[:, DIFF_VD:])
                o = merged(s_self, v_self, m_d, l_d, acc_d)
                lam = _lam(lq1_ref, lk1_ref, lq2_ref, lk2_ref, lam_init)
                d = o[:DIFF_HEADS] - lam * o[DIFF_HEADS:]
                ob_ref[b] = _rms(d, gsub_ref[...]) * (1.0 - lam_init)

        def make(step):
            def run():
                if step == 0:
                    begin()
                if step < DEC_CHAINS:
                    prods[step] = score_products(step)
                c = step - DEC_SKEW
                if c >= 0:
                    ckv, res, bp, sd = prods.pop(c)
                    value_products(c, ckv, *softmax(c, res, bp, sd))
                if step == n_stages - 1:
                    finish()
            return run

        return [make(step) for step in range(n_stages)]

    def prime():
        lhs_ref[:nk, :] = wukt_ref[...]
        start_step(0, 0)

    def drain():
        wait_step(total & 1)

    return prime, drain, stages, total


def _dec_scratch(caches):
    return ([pltpu.VMEM((2, DEC_PAGES) + c.shape[2:], c.dtype) for c in caches]
            + [pltpu.SemaphoreType.DMA((len(caches), 2)),
               pltpu.VMEM((MLA_HEADS * MLA_NOPE + 16, LANES), BF16),
               pltpu.VMEM((DEC_CHAINS, MLA_HEADS, 1), F32), pltpu.VMEM((DEC_CHAINS, MLA_HEADS, 1), F32),
               pltpu.VMEM((DEC_CHAINS, MLA_HEADS, MLA_KV_RANK), F32),
               pltpu.VMEM((DEC_CHAINS, MLA_HEADS, 1), F32), pltpu.VMEM((DEC_CHAINS, MLA_HEADS, 1), F32),
               pltpu.VMEM((DEC_CHAINS, MLA_HEADS, DIFF_VD), F32)])


def _dec_kernel(*refs, layer, n_seq, n_pages, lam_init):
    prime, drain, stages, total = _dec_pipeline(
        refs, layer=layer, n_seq=n_seq, n_pages=n_pages, lam_init=lam_init)

    def body(t, carry):
        for run in stages(t):
            run()
        return carry

    prime()
    lax.fori_loop(0, total, body, 0)
    drain()


def _decode(page_table, per_seq, consts, caches, layer, lam_init):
    b, n_pages = page_table.shape

    def full_spec(a):
        return pl.BlockSpec(a.shape, lambda i, pt: (0,) * a.ndim)

    oa_w = MLA_HEADS * MLA_V
    out_shape = [jax.ShapeDtypeStruct((b, 1, oa_w), F32), jax.ShapeDtypeStruct((b, DIFF_HEADS, DIFF_VD), F32)]
    grid_spec = pltpu.PrefetchScalarGridSpec(
        num_scalar_prefetch=1,
        grid=(1,),
        in_specs=[full_spec(a) for a in list(per_seq) + list(consts)]
                 + [pl.BlockSpec(memory_space=pl.ANY)] * len(caches),
        out_specs=[full_spec(o) for o in out_shape],
        scratch_shapes=_dec_scratch(caches),
    )
    return pl.pallas_call(
        functools.partial(_dec_kernel, layer=layer, n_seq=b, n_pages=n_pages, lam_init=lam_init),
        grid_spec=grid_spec,
        out_shape=out_shape,
        compiler_params=_cparams(("arbitrary",)),
        name="decode",
    )(page_table, *per_seq, *consts, *caches)


def _out_kernel(mix_ref, x_ref, gt_ref, sc_ref, sh_ref, wo_ref, g2_ref, wrh_ref, wrl_ref, br_ref,
                x1_ref, h2_ref, gate_ref):
    o = _dot(mix_ref[...], wo_ref[...])
    x1 = x_ref[...] + gt_ref[...] * o
    x1_ref[...] = x1
    h2 = _rms(x1, g2_ref[...]) * (1.0 + sc_ref[...]) + sh_ref[...]
    h2_ref[...] = h2.astype(BF16)
    hh, hl = _split(h2)
    logits = _dot(hh, wrh_ref[...]) + _dot(hh, wrl_ref[...]) + _dot(hl, wrh_ref[...]) + br_ref[...]
    lane_i = lax.broadcasted_iota(jnp.int32, logits.shape, 1)
    lane = lane_i.astype(F32)
    big = float(ROUTER_LANES)
    gl = jnp.where(lane_i < N_GROUPS, logits, NEG)
    gmax = jnp.max(gl, axis=-1, keepdims=True)
    gidx = jnp.min(jnp.where(gl == gmax, lane, big), axis=-1, keepdims=True)
    g_w = 1.0 / jnp.sum(jnp.exp(gl - gmax), axis=-1, keepdims=True)
    in_group = (lane_i >= N_GROUPS) & (lane_i < N_GROUPS + N_ROUTED) & (
        ((lane_i - N_GROUPS) >> 3).astype(F32) == gidx)
    el = jnp.where(in_group, logits, NEG)
    e1 = jnp.max(el, axis=-1, keepdims=True)
    i1 = jnp.min(jnp.where(el == e1, lane, big), axis=-1, keepdims=True)
    el2 = jnp.where(lane == i1, NEG, el)
    e2 = jnp.max(el2, axis=-1, keepdims=True)
    i2 = jnp.min(jnp.where(el2 == e2, lane, big), axis=-1, keepdims=True)
    t = jnp.exp(e2 - e1)
    w1 = 1.0 / (1.0 + t)
    w2 = t / (1.0 + t)
    gate_ref[...] = jnp.where(lane == i1, w1, jnp.where(lane == i2, w2, 0.0)) * g_w


def _out(mix, x3, gt, sc, sh, wo, g2, wrh, wrl, br, tm):
    b, t, d = x3.shape
    per_tok = gt.shape[1] != 1

    def tok_spec(w):
        return pl.BlockSpec((None, tm, w), lambda s, bb: (bb, s, 0))

    mod_spec = tok_spec(d) if per_tok else pl.BlockSpec((None, 1, d), lambda s, bb: (bb, 0, 0))
    consts = [wo, g2, wrh, wrl, br]
    return pl.pallas_call(
        _out_kernel,
        grid=(t // tm, b),
        in_specs=[tok_spec(mix.shape[2]), tok_spec(d), mod_spec, mod_spec, mod_spec] + [_const_spec(c.shape) for c in consts],
        out_specs=[tok_spec(d), tok_spec(d), tok_spec(ROUTER_LANES)],
        out_shape=[jax.ShapeDtypeStruct((b, t, d), F32), jax.ShapeDtypeStruct((b, t, d), BF16),
                   jax.ShapeDtypeStruct((b, t, ROUTER_LANES), F32)],
        compiler_params=_cparams(("parallel", "parallel")),
        name="out",
    )(mix, x3, gt, sc, sh, *consts)


MOE_CHUNK = MXU_DIM


def _moe_kernel(h2_ref, gate_ref, x1_ref, gt_ref, wg_ref, wu_ref, wd_ref, ex_ref, y_ref):
    g = pl.program_id(2)

    @pl.when(g == 0)
    def _():
        y_ref[...] = jnp.zeros(y_ref.shape, F32)

    h = h2_ref[...]
    gate_b = gate_ref[...].astype(BF16)
    for c in range(GROUP_FF // MOE_CHUNK):
        cs = slice(c * MOE_CHUNK, (c + 1) * MOE_CHUNK)
        ge = _dot(gate_b, ex_ref[:, cs])
        a = _silu(_dot(h, wg_ref[:, cs])) * _dot(h, wu_ref[:, cs]) * ge
        y_ref[...] += _dot(a.astype(BF16), wd_ref[cs, :])

    @pl.when(g == N_GROUPS - 1)
    def _():
        y_ref[...] = x1_ref[...] + gt_ref[...] * y_ref[...]


def _moe(h2, gate, x1, gt, wg, wu, wd, ex, tm):
    b, t, d = x1.shape
    per_tok = gt.shape[1] != 1

    def tok_spec(w):
        return pl.BlockSpec((None, tm, w), lambda s, bb, g: (bb, s, 0))

    mod_spec = tok_spec(d) if per_tok else pl.BlockSpec((None, 1, d), lambda s, bb, g: (bb, 0, 0))

    def grp_spec(a):
        return pl.BlockSpec((None,) + a.shape[1:], lambda s, bb, g: (g, 0, 0))

    return pl.pallas_call(
        _moe_kernel,
        grid=(t // tm, b, N_GROUPS),
        in_specs=[tok_spec(d), tok_spec(ROUTER_LANES), tok_spec(d), mod_spec,
                  grp_spec(wg), grp_spec(wu), grp_spec(wd), grp_spec(ex)],
        out_specs=tok_spec(d),
        out_shape=jax.ShapeDtypeStruct((b, t, d), F32),
        compiler_params=_cparams(("parallel", "parallel", "arbitrary")),
        name="moe",
    )(h2, gate, x1, gt, wg, wu, wd, ex)


def _token_tiles(seq):
    return min(512, seq), min(512, seq), min(512, seq), min(1024, seq)


def _rope_tables(pos):
    def cs(dim):
        half = dim // 2
        inv = ROPE_THETA ** (-jnp.arange(half, dtype=F32) * 2.0 / dim)
        ang = pos[:, None] * inv[None, :]
        c, s = jnp.cos(ang), jnp.sin(ang)
        return jnp.concatenate([c, c], axis=1), jnp.concatenate([-s, s], axis=1)

    t = pos.shape[0]
    c32, s32 = cs(MLA_ROPE)
    c64, s64 = cs(DIFF_HD)
    z = lambda w: jnp.zeros((t, w), F32)
    pad = LANES - MLA_NOPE - MLA_ROPE
    cosq = jnp.concatenate([jnp.ones((t, MLA_NOPE), F32), c32, z(pad)], axis=1)
    sinq = jnp.concatenate([z(MLA_NOPE), s32, z(pad)], axis=1)
    cosk = jnp.concatenate([c32, z(LANES - MLA_ROPE)], axis=1)
    sink = jnp.concatenate([s32, z(LANES - MLA_ROPE)], axis=1)
    cosd = jnp.concatenate([c64, c64], axis=1)
    sind = jnp.concatenate([s64, s64], axis=1)
    return [cosq, sinq, cosk, sink, cosd, sind]


def _block_diag_mean(sizes, width):
    m = jnp.zeros((width, width), F32)
    o = 0
    while o < width:
        for sz in sizes:
            if sz > 0:
                m = m.at[o:o + sz, o:o + sz].set(1.0 / sz)
            o += abs(sz)
    return m.astype(BF16)


def _layer_weights(l, w_in, g_norm1, g_mla_qa, w_mla_uq, g_mla_kva, w_mla_uk, g_mla_qn_nope, g_mla_qn_rope,
                   g_mla_kn_nope, g_mla_kn_rope, g_diff_qn, g_diff_kn):
    d = w_in.shape[1]
    o_kpe = MLA_Q_RANK + MLA_KV_RANK
    wi = w_in[l]
    win = jnp.concatenate([wi[:, :o_kpe], wi[:, o_kpe:o_kpe + MLA_ROPE], jnp.zeros((d, LANES - MLA_ROPE), F32),
                           wi[:, o_kpe + MLA_ROPE:]], axis=1).astype(BF16)
    pad = LANES - MLA_NOPE - MLA_ROPE
    wuq = w_mla_uq[l].reshape(MLA_Q_RANK, MLA_HEADS, MLA_NOPE + MLA_ROPE)
    wuq = jnp.concatenate([wuq, jnp.zeros((MLA_Q_RANK, MLA_HEADS, pad), F32)], axis=2).reshape(MLA_Q_RANK, QK_W).astype(BF16)
    wuk = jnp.concatenate([w_mla_uk[l], jnp.zeros((MLA_KV_RANK, MLA_HEADS, LANES - MLA_NOPE), F32)], axis=2)
    wuk = wuk.reshape(MLA_KV_RANK, QK_W).astype(BF16)
    gq = jnp.tile(jnp.concatenate([g_mla_qn_nope[l], g_mla_qn_rope[l], jnp.zeros((pad,), F32)]), MLA_HEADS)[None]
    gk = jnp.tile(jnp.concatenate([g_mla_kn_nope[l], jnp.zeros((LANES - MLA_NOPE,), F32)]), MLA_HEADS)[None]
    gkpe = jnp.concatenate([g_mla_kn_rope[l], jnp.zeros((LANES - MLA_ROPE,), F32)])[None]
    gdq = jnp.tile(g_diff_qn[l], DQ_W // DIFF_HD)[None]
    gdk = jnp.tile(g_diff_kn[l], DK_W // DIFF_HD)[None]
    bdq = _block_diag_mean((MLA_NOPE, MLA_ROPE, -pad), MXU_DIM)
    bdd = _block_diag_mean((DIFF_HD,), MXU_DIM)
    return [g_norm1[l][None], win, g_mla_qa[l][None], wuq, gq, g_mla_kva[l][None], wuk, gk, gkpe, gdq, gdk, bdq, bdd]


def kernel(x_prompt, x_sample, cache_mla_ckv, cache_mla_kpe, cache_diff_k, cache_diff_v, page_table, c_prompt, c_sample, w_ada, b_ada, g_norm1, w_in, g_mla_qa, w_mla_uq, g_mla_kva, w_mla_uk, w_mla_uv, g_mla_qn_nope, g_mla_qn_rope, g_mla_kn_nope, g_mla_kn_rope, g_diff_qn, g_diff_kn, lam_q1, lam_k1, lam_q2, lam_k2, g_diff_subln, w_o, g_norm2, w_router_group, b_router_group, w_router_expert, b_router_expert, w_exp_gate, w_exp_up, w_exp_down):
    bp, sp, d = x_prompt.shape
    bs, ts, _ = x_sample.shape
    depth = w_in.shape[0]
    n_pool = cache_mla_ckv.shape[1]
    n_pages = page_table.shape[1]
    assert ts == 1 and n_pages % DEC_PAGES == 0 and cache_mla_ckv.shape[2] == PAGE_SIZE
    assert w_in.shape[2] == PROJ_W - LANES + MLA_ROPE and d % MXU_DIM == 0
    past = n_pages * PAGE_SIZE

    tm_p, tq, tm_o, tm_m = _token_tiles(sp)
    tables_p = _rope_tables(jnp.arange(sp, dtype=F32))
    tables_s = _rope_tables(jnp.arange(ts, dtype=F32) + past)

    kpe_t = jnp.transpose(cache_mla_kpe, (0, 1, 3, 2))
    k_t = jnp.transpose(cache_diff_k, (0, 1, 3, 4, 5, 2)).reshape(depth, n_pool, DK_W, PAGE_SIZE)
    v_rows = cache_diff_v.reshape(depth, n_pool, PAGE_SIZE * DIFF_KV_HEADS, DIFF_VD)

    xp = x_prompt
    xs = x_sample.reshape(1, bs, d)
    outs_p = [[], [], [], []]
    outs_s = [[], [], [], []]
    for l in range(depth):
        lam_init = 0.8 - 0.6 * math.exp(-0.3 * l)
        wts = _layer_weights(l, w_in, g_norm1, g_mla_qa, w_mla_uq, g_mla_kva, w_mla_uk, g_mla_qn_nope,
                             g_mla_qn_rope, g_mla_kn_nope, g_mla_kn_rope, g_diff_qn, g_diff_kn)
        lams = [lam_q1[l][None], lam_k1[l][None], lam_q2[l][None], lam_k2[l][None]]
        gsub = g_diff_subln[l][None]
        wuv = w_mla_uv[l]
        wuv_flat = wuv.reshape(MLA_KV_RANK, MLA_HEADS * MLA_V).astype(BF16)
        wuvt = jnp.transpose(wuv, (1, 2, 0)).astype(BF16)
        wuk_t = jnp.concatenate([jnp.transpose(w_mla_uk[l], (1, 2, 0)),
                                 jnp.zeros((MLA_HEADS, LANES - MLA_NOPE, MLA_KV_RANK), F32)], axis=1).astype(BF16)
        wukt_rows = jnp.transpose(w_mla_uk[l], (1, 2, 0)).reshape(MLA_HEADS * MLA_NOPE, MLA_KV_RANK).astype(BF16)
        gk_row = jnp.concatenate([g_mla_kn_nope[l], jnp.zeros((LANES - MLA_NOPE,), F32)])[None]
        wo = w_o[l].astype(BF16)
        wr = jnp.concatenate([w_router_group[l], jnp.transpose(w_router_expert[l], (1, 0, 2)).reshape(d, N_ROUTED),
                              jnp.zeros((d, ROUTER_LANES - N_GROUPS - N_ROUTED), F32)], axis=1)
        wrh, wrl = _split(wr)
        br = jnp.concatenate([b_router_group[l], b_router_expert[l].reshape(N_ROUTED),
                              jnp.zeros((ROUTER_LANES - N_GROUPS - N_ROUTED,), F32)])[None]
        wg = jnp.transpose(w_exp_gate[l], (0, 2, 1, 3)).reshape(N_GROUPS, d, GROUP_FF).astype(BF16)
        wu = jnp.transpose(w_exp_up[l], (0, 2, 1, 3)).reshape(N_GROUPS, d, GROUP_FF).astype(BF16)
        wd = w_exp_down[l].reshape(N_GROUPS, GROUP_FF, d).astype(BF16)
        lane = jnp.arange(ROUTER_LANES)[None, :, None]
        col = jnp.arange(GROUP_FF)[None, None, :]
        grp = jnp.arange(N_GROUPS)[:, None, None]
        ex = (lane == N_GROUPS + grp * EXPERTS_PER_GROUP + col // EXPERT_FF).astype(BF16)

        mod = _ada(jnp.concatenate([c_prompt, c_sample], axis=0), w_ada[l], b_ada[l][None])
        mod_p = mod[:bp].reshape(bp, 6, 1, d)
        mod_s = mod[bp:].reshape(1, bs, 6, d)
        sh1p, sc1p, gt1p, sh2p, sc2p, gt2p = [mod_p[:, k] for k in range(6)]
        sh1s, sc1s, gt1s, sh2s, sc2s, gt2s = [mod_s[:, :, k] for k in range(6)]

        (qf, kf, dq0, dq1, ckvt, dkb, dvt, ckv, kpe, dk, dv) = _proj(xp, sc1p, sh1p, tables_p, wts, tm_p)
        mix = _attn(qf, dq0, dq1, kf, ckvt, dkb, dvt, wuvt, gsub.reshape(DIFF_VD, 1), lams, lam_init, tq)
        x1, h2, gate = _out(mix, xp, gt1p, sc2p, sh2p, wo, g_norm2[l][None], wrh, wrl, br, tm_o)
        for lst, a in zip(outs_p, (ckv, kpe, dk, dv)):
            lst.append(a)
        (qf, kf, dq0, dq1, _, dkb, _, ckv, kpe, dk, dv) = _proj(xs, sc1s, sh1s, tables_s, wts, bs)
        for lst, a in zip(outs_s, (ckv, kpe, dk, dv)):
            lst.append(a)
        qf2 = qf.reshape(bs, QK_W)
        qa = _qabs(qf2, gk_row, wuk_t).reshape(bs, MLA_HEADS, LANES)
        qf8 = qf2.reshape(bs, MLA_HEADS, LANES)
        kf8 = kf.reshape(bs, MLA_HEADS, LANES)
        qp = qf8[:, :, MLA_NOPE:MLA_NOPE + MLA_ROPE]
        dq5 = (dq0 + dq1).reshape(bs, DIFF_KV_HEADS, 2, 2, DIFF_HD)
        eye = jnp.eye(2, dtype=BF16)
        qbd = jnp.einsum('bgrmd,gh,mn->bmgrhnd', dq5, eye, eye).reshape(bs, 2 * DIFF_HEADS, DK_W)
        per_seq = [qa, qp, qbd, qf8, kf8, dkb.reshape(bs, 1, DK_W), ckv.reshape(bs, 1, MLA_KV_RANK),
                   dv.reshape(bs, 1, DV_W)]
        consts = [wukt_rows, wuv_flat, gsub] + lams
        caches = (cache_mla_ckv, kpe_t, k_t, v_rows)

        xp = _moe(h2, gate, x1, gt2p, wg, wu, wd, ex, tm_m)
        oa, ob = _decode(page_table, per_seq, consts, caches, l, lam_init)
        mix = jnp.concatenate([oa.reshape(bs, -1), ob.reshape(bs, -1)], axis=1).astype(BF16).reshape(1, bs, -1)
        x1, h2, gate = _out(mix, xs, gt1s, sc2s, sh2s, wo, g_norm2[l][None], wrh, wrl, br, bs)
        xs = _moe(h2, gate, x1, gt2s, wg, wu, wd, ex, bs)

    def stack_p(lst, tail):
        return jnp.stack(lst).reshape((depth, bp, sp) + tail)

    def stack_s(lst, tail):
        return jnp.stack(lst).reshape((depth, bs, ts) + tail)

    k_tail = (DIFF_KV_HEADS, 2, DIFF_HD)
    v_tail = (DIFF_KV_HEADS, DIFF_VD)
    return (xp, xs.reshape(bs, ts, d),
            stack_p(outs_p[0], (MLA_KV_RANK,)), stack_p(outs_p[1], (MLA_ROPE,)), stack_p(outs_p[2], k_tail),
            stack_p(outs_p[3], v_tail),
            stack_s(outs_s[0], (MLA_KV_RANK,)), stack_s(outs_s[1], (MLA_ROPE,)), stack_s(outs_s[2], k_tail),
            stack_s(outs_s[3], v_tail))
```

```python
import functools
import math

import jax
import jax.numpy as jnp
from jax import lax
from jax.experimental import pallas as pl
from jax.experimental.pallas import tpu as pltpu

F32 = jnp.float32
BF16 = jnp.bfloat16

MLA_HEADS = 8
MLA_Q_RANK = 256
MLA_KV_RANK = 128
MLA_NOPE = 64
MLA_ROPE = 32
MLA_V = 64
DIFF_HEADS = 4
DIFF_KV_HEADS = 2
DIFF_HD = 64
DIFF_VD = 128
N_GROUPS = 4
EXPERTS_PER_GROUP = 8
EXPERT_FF = 128
PAGE_SIZE = 128
ROPE_THETA = 10000.0
EPS = 1e-6
LOG2E = 1.4426950408889634
MLA_SCALE = (MLA_NOPE + MLA_ROPE) ** -0.5
DIFF_SCALE = DIFF_HD ** -0.5
NEG = -1e30

LANES = 128
MXU_DIM = 256
VMEM_LIMIT = 56 * 1024 * 1024
ROUTER_LANES = 128
N_ROUTED = N_GROUPS * EXPERTS_PER_GROUP
GROUP_FF = EXPERTS_PER_GROUP * EXPERT_FF
QK_W = MLA_HEADS * LANES
DQ_W = DIFF_HEADS * 2 * DIFF_HD
DK_W = DIFF_KV_HEADS * 2 * DIFF_HD
DV_W = DIFF_KV_HEADS * DIFF_VD
PROJ_W = MLA_Q_RANK + MLA_KV_RANK + LANES + DQ_W + DK_W + DV_W
DEC_PAGES = 32
DEC_CHAIN_PAGES = 2
DEC_CHAINS = DEC_PAGES // DEC_CHAIN_PAGES
DEC_SKEW = 2


def _dot(a, b):
    return jnp.dot(a, b, preferred_element_type=F32)


def _dot_nt(a, b):
    return lax.dot_general(a, b, (((1,), (1,)), ((), ())), preferred_element_type=F32)


def _split(a):
    hi = a.astype(BF16)
    lo = (a - hi.astype(F32)).astype(BF16)
    return hi, lo


def _rms(v, g):
    return v * lax.rsqrt(jnp.mean(v * v, axis=-1, keepdims=True) + EPS) * g


def _silu(v):
    return v / (1.0 + jnp.exp(-v))


def _cparams(sem):
    return pltpu.CompilerParams(dimension_semantics=sem, vmem_limit_bytes=VMEM_LIMIT)


def _const_spec(shape):
    nd = len(shape)
    return pl.BlockSpec(shape, lambda *_: (0,) * nd)


def _ada_kernel(c_ref, w_ref, b_ref, o_ref):
    s = _silu(c_ref[...])
    sh, sl = _split(s)
    wh, wl = _split(w_ref[...])
    o_ref[...] = _dot(sh, wh) + _dot(sh, wl) + _dot(sl, wh) + b_ref[...]


def _ada(c, w, b):
    m, d = c.shape
    n = w.shape[1]
    tn = 512
    return pl.pallas_call(
        _ada_kernel,
        grid=(n // tn,),
        in_specs=[_const_spec((m, d)), pl.BlockSpec((d, tn), lambda i: (0, i)), pl.BlockSpec((1, tn), lambda i: (0, i))],
        out_specs=pl.BlockSpec((m, tn), lambda i: (0, i)),
        out_shape=jax.ShapeDtypeStruct((m, n), F32),
        compiler_params=_cparams(("parallel",)),
        name="ada",
    )(c, w, b)


def _block_norm(v, bd, g):
    w = v.shape[1]
    sq = (v * v).astype(BF16)
    ms = jnp.concatenate([_dot(sq[:, i:i + MXU_DIM], bd) for i in range(0, w, MXU_DIM)], axis=1)
    return v * lax.rsqrt(ms + EPS) * g


def _rope(v, cos, sin, half, first):
    parts = []
    for i in range(0, v.shape[1], LANES):
        s = v[:, i:i + LANES]
        rot = jnp.where(first, pltpu.roll(s, LANES - half, 1), pltpu.roll(s, half, 1))
        parts.append(s * cos + rot * sin)
    return parts[0] if len(parts) == 1 else jnp.concatenate(parts, axis=1)


def _proj_kernel(x_ref, sc_ref, sh_ref, cosq_ref, sinq_ref, cosk_ref, sink_ref, cosd_ref, sind_ref,
                 g1_ref, win_ref, gqa_ref, wuq_ref, gq_ref, gkva_ref, wuk_ref, gk_ref, gkpe_ref, gdq_ref, gdk_ref,
                 bdq_ref, bdd_ref,
                 qf_ref, kf_ref, dq0_ref, dq1_ref, ckvt_ref, dkb_ref, dvt_ref, ckv_ref, kpe_ref, dk_ref, dv_ref):
    lane = lax.broadcasted_iota(jnp.int32, (1, LANES), 1)
    x = x_ref[...]
    h = _rms(x, g1_ref[...]) * (1.0 + sc_ref[...]) + sh_ref[...]
    proj = _dot(h.astype(BF16), win_ref[...])
    o_ckv = MLA_Q_RANK
    o_kpe = o_ckv + MLA_KV_RANK
    o_dq = o_kpe + LANES
    o_dk = o_dq + DQ_W
    o_dv = o_dk + DK_W

    cqn = _rms(proj[:, :MLA_Q_RANK], gqa_ref[...])
    q = _dot(cqn.astype(BF16), wuq_ref[...])
    qn = _block_norm(q, bdq_ref[...], gq_ref[...])
    q_first = (lane >= MLA_NOPE) & (lane < MLA_NOPE + MLA_ROPE // 2)
    qf = _rope(qn, cosq_ref[...], sinq_ref[...], MLA_ROPE // 2, q_first)
    qf_ref[...] = (qf * (MLA_SCALE * LOG2E)).astype(BF16)

    ckv = _rms(proj[:, o_ckv:o_kpe], gkva_ref[...])
    ckv_ref[...] = ckv
    ckvb = ckv.astype(BF16)
    ckvt_ref[...] = ckv.T.astype(BF16)
    kr = proj[:, o_kpe:o_dq]
    kn = kr * lax.rsqrt(jnp.sum(kr * kr, axis=-1, keepdims=True) * (1.0 / MLA_ROPE) + EPS) * gkpe_ref[...]
    kpe = _rope(kn, cosk_ref[...], sink_ref[...], MLA_ROPE // 2, lane < MLA_ROPE // 2)
    kpe_ref[...] = kpe[:, :MLA_ROPE]

    kraw = _dot(ckvb, wuk_ref[...])
    knn = _block_norm(kraw, bdq_ref[...], gk_ref[...])
    kpe_at_rope = pltpu.roll(kpe, MLA_NOPE, 1)
    kf_ref[...] = jnp.concatenate(
        [knn[:, i:i + LANES] + kpe_at_rope for i in range(0, QK_W, LANES)], axis=1).astype(BF16)

    d_first = (lane & (DIFF_HD - 1)) < DIFF_HD // 2
    dq = _block_norm(proj[:, o_dq:o_dk], bdd_ref[...], gdq_ref[...])
    dq = _rope(dq, cosd_ref[...], sind_ref[...], DIFF_HD // 2, d_first) * (DIFF_SCALE * LOG2E)
    map0 = (lax.broadcasted_iota(jnp.int32, (1, DQ_W), 1) & (LANES - 1)) < DIFF_HD
    dq0_ref[...] = jnp.where(map0, dq, 0.0).astype(BF16)
    dq1_ref[...] = jnp.where(map0, 0.0, dq).astype(BF16)
    dk = _block_norm(proj[:, o_dk:o_dv], bdd_ref[...], gdk_ref[...])
    dk = _rope(dk, cosd_ref[...], sind_ref[...], DIFF_HD // 2, d_first)
    dk_ref[...] = dk
    dkb_ref[...] = dk.astype(BF16)
    dv = proj[:, o_dv:]
    dv_ref[...] = dv
    dvt_ref[...] = dv.T.astype(BF16)


def _proj(x3, sc, sh, tables, wts, tm):
    b, t, d = x3.shape
    nt = t // tm
    per_tok = sc.shape[1] != 1
    tab_rows = tables[0].shape[0]

    def tok_spec(w):
        return pl.BlockSpec((None, tm, w), lambda s, bb: (bb, s, 0))

    mod_spec = tok_spec(d) if per_tok else pl.BlockSpec((None, 1, d), lambda s, bb: (bb, 0, 0))
    tab_spec = (pl.BlockSpec((tm, LANES), lambda s, bb: (s, 0)) if tab_rows != 1
                else pl.BlockSpec((1, LANES), lambda s, bb: (0, 0)))
    in_specs = [tok_spec(d), mod_spec, mod_spec] + [tab_spec] * 6 + [_const_spec(w.shape) for w in wts]
    outs = [(QK_W, BF16, False), (QK_W, BF16, False), (DQ_W, BF16, False), (DQ_W, BF16, False),
            (MLA_KV_RANK, BF16, True), (DK_W, BF16, False), (DV_W, BF16, True),
            (MLA_KV_RANK, F32, False), (MLA_ROPE, F32, False), (DK_W, F32, False), (DV_W, F32, False)]

    def out_spec(w, tr):
        return pl.BlockSpec((None, w, tm), lambda s, bb: (bb, 0, s)) if tr else tok_spec(w)

    return pl.pallas_call(
        _proj_kernel,
        grid=(nt, b),
        in_specs=in_specs,
        out_specs=[out_spec(w, tr) for w, _, tr in outs],
        out_shape=[jax.ShapeDtypeStruct((b, w, t) if tr else (b, t, w), dt) for w, dt, tr in outs],
        compiler_params=_cparams(("parallel", "parallel")),
        name="proj",
    )(x3, sc, sh, *tables, *wts)


def _lam(lq1_ref, lk1_ref, lq2_ref, lk2_ref, lam_init):
    a = jnp.sum(lq1_ref[...] * lk1_ref[...], axis=-1, keepdims=True)
    b = jnp.sum(lq2_ref[...] * lk2_ref[...], axis=-1, keepdims=True)
    return jnp.exp(a) - jnp.exp(b) + lam_init


ATTN_MAPS = MLA_HEADS + 2 * DIFF_HEADS
ATTN_SLOTS = 2


def _col_tree(x, op):
    parts = [x[c * 64:(c + 1) * 64] for c in range(x.shape[0] // 64)]
    while len(parts) > 1:
        parts = [op(parts[a], parts[a + 1]) for a in range(0, len(parts), 2)]
    return parts[0]


def _attn_kernel(qf_ref, dq0_ref, dq1_ref, kf_ref, ckvt_ref, dk_ref, dvt_ref, wuvt_ref, gsub_ref,
                 lq1_ref, lk1_ref, lq2_ref, lk2_ref, out_ref, m_s, l_s, acc_s, s_scr, *, tq, lam_init):
    i = pl.program_id(1)
    j = pl.program_id(2)

    @pl.when(j == 0)
    def _():
        m_s[...] = jnp.full(m_s.shape, NEG, F32)
        l_s[...] = jnp.zeros(l_s.shape, F32)
        acc_s[...] = jnp.zeros(acc_s.shape, F32)

    maps = []
    for h in range(MLA_HEADS):
        sl = slice(h * LANES, (h + 1) * LANES)
        maps.append((qf_ref, sl, kf_ref, sl, None, h))
    for g in range(DIFF_KV_HEADS):
        gs = slice(g * LANES, (g + 1) * LANES)
        for r in range(DIFF_HEADS // DIFF_KV_HEADS):
            sl = slice((g * 2 + r) * LANES, (g * 2 + r + 1) * LANES)
            maps.append((dq0_ref, sl, dk_ref, gs, gs, MLA_HEADS + (g * 2 + r) * 2))
            maps.append((dq1_ref, sl, dk_ref, gs, gs, MLA_HEADS + (g * 2 + r) * 2 + 1))

    def scores(n, pieces, masked):
        q_ref, qs, k_ref, ks, _, _ = maps[n]
        for q0, q1, nk in pieces:
            st = _dot_nt(k_ref[:nk, ks], q_ref[q0:q1, qs])
            if masked:
                key = lax.broadcasted_iota(jnp.int32, st.shape, 0)
                qry = lax.broadcasted_iota(jnp.int32, st.shape, 1) + q0
                st = jnp.where(key <= qry, st, NEG)
            s_scr[n % ATTN_SLOTS, :nk, q0:q1] = st

    def softmax_pv(n, pieces):
        _, _, _, _, vs, idx = maps[n]
        for q0, q1, nk in pieces:
            vt = ckvt_ref[:, :nk] if vs is None else dvt_ref[vs, :nk]
            st = s_scr[n % ATTN_SLOTS, :nk, q0:q1]
            m_prev = m_s[idx, :, q0:q1]
            m_new = jnp.maximum(m_prev, jnp.max(_col_tree(st, jnp.maximum), axis=0, keepdims=True))
            alpha = jnp.exp2(m_prev - m_new)
            p = jnp.exp2(st - m_new)
            l_s[idx, :, q0:q1] = alpha * l_s[idx, :, q0:q1] + jnp.sum(_col_tree(p, jnp.add), axis=0, keepdims=True)
            acc_s[idx, :, q0:q1] = alpha * acc_s[idx, :, q0:q1] + _dot(vt, p.astype(BF16))
            m_s[idx, :, q0:q1] = m_new

    def step(masked):
        half = tq // 2
        pieces = [(0, half, half), (half, tq, tq)] if masked else [(0, tq, tq)]
        scores(0, pieces, masked)
        for n in range(ATTN_MAPS):
            if n + 1 < ATTN_MAPS:
                scores(n + 1, pieces, masked)
            softmax_pv(n, pieces)

    @pl.when(j < i)
    def _():
        step(False)

    @pl.when(j == i)
    def _():
        step(True)
        outs = []
        for h in range(MLA_HEADS):
            lat_t = (acc_s[h] / l_s[h]).astype(BF16)
            outs.append(_dot(wuvt_ref[h], lat_t))
        lam = _lam(lq1_ref, lk1_ref, lq2_ref, lk2_ref, lam_init)
        for gr in range(DIFF_HEADS):
            i0 = MLA_HEADS + 2 * gr
            d = acc_s[i0] / l_s[i0] - lam * (acc_s[i0 + 1] / l_s[i0 + 1])
            d = d * lax.rsqrt(jnp.mean(d * d, axis=0, keepdims=True) + EPS) * gsub_ref[...]
            outs.append(d * (1.0 - lam_init))
        out_ref[...] = jnp.concatenate(outs, axis=0).T.astype(out_ref.dtype)


def _attn(qf, dq0, dq1, kf, ckvt, dkb, dvt, wuvt, gsub_col, lams, lam_init, tq):
    b, s, _ = qf.shape
    nq = s // tq
    assert s % tq == 0 and tq % (2 * LANES) == 0 and (tq // LANES) & (tq // LANES - 1) == 0

    def q_spec(w):
        return pl.BlockSpec((None, tq, w), lambda bb, i, j: (bb, i, 0))

    def k_spec(w):
        return pl.BlockSpec((None, tq, w), lambda bb, i, j: (bb, jnp.minimum(i, j), 0))

    def kt_spec(w):
        return pl.BlockSpec((None, w, tq), lambda bb, i, j: (bb, 0, jnp.minimum(i, j)))

    mix_w = MLA_HEADS * MLA_V + DIFF_HEADS * DIFF_VD
    return pl.pallas_call(
        functools.partial(_attn_kernel, tq=tq, lam_init=lam_init),
        grid=(b, nq, nq),
        in_specs=[q_spec(QK_W), q_spec(DQ_W), q_spec(DQ_W), k_spec(QK_W), kt_spec(MLA_KV_RANK), k_spec(DK_W),
                  kt_spec(DV_W), _const_spec(wuvt.shape), _const_spec(gsub_col.shape)]
                 + [_const_spec(l.shape) for l in lams],
        out_specs=q_spec(mix_w),
        out_shape=jax.ShapeDtypeStruct((b, s, mix_w), BF16),
        scratch_shapes=[pltpu.VMEM((ATTN_MAPS, 1, tq), F32), pltpu.VMEM((ATTN_MAPS, 1, tq), F32),
                        pltpu.VMEM((ATTN_MAPS, DIFF_VD, tq), F32), pltpu.VMEM((ATTN_SLOTS, tq, tq), F32)],
        compiler_params=_cparams(("parallel", "parallel", "arbitrary")),
        name="attn",
    )(qf, dq0, dq1, kf, ckvt, dkb, dvt, wuvt, gsub_col, *lams)


def _qabs_kernel(qf_ref, gk_ref, wt_ref, qa_ref):
    for h in range(MLA_HEADS):
        sl = slice(h * LANES, (h + 1) * LANES)
        qg = (qf_ref[:, sl].astype(F32) * gk_ref[...]).astype(BF16)
        qa_ref[:, sl] = _dot(qg, wt_ref[h])


def _qabs(qf, gk_row, wuk_t):
    n = qf.shape[0]
    return pl.pallas_call(
        _qabs_kernel,
        grid=(1,),
        in_specs=[_const_spec(qf.shape), _const_spec(gk_row.shape), _const_spec(wuk_t.shape)],
        out_specs=_const_spec((n, QK_W)),
        out_shape=jax.ShapeDtypeStruct((n, QK_W), F32),
        compiler_params=_cparams(("arbitrary",)),
        name="qabs",
    )(qf, gk_row, wuk_t)


def _dec_pipeline(refs, *, layer, n_seq, n_pages, lam_init):
    (pt_ref, qa_ref, qp_ref, qbd_ref, qf8_ref, kf8_ref, dkrow_ref, ckvrow_ref, dvrow_ref,
     wukt_ref, wuv_ref, gsub_ref, lq1_ref, lk1_ref, lq2_ref, lk2_ref,
     ckv_hbm, kpe_hbm, kt_hbm, v_hbm, oa_ref, ob_ref,
     ckv_buf, kpe_buf, kt_buf, v_buf, sems, lhs_ref, m_a, l_a, acc_a, m_d, l_d, acc_d) = refs
    pp = DEC_PAGES
    n_steps = n_pages // pp
    total = n_seq * n_steps
    hbm = (ckv_hbm, kpe_hbm, kt_hbm, v_hbm)
    bufs = (ckv_buf, kpe_buf, kt_buf, v_buf)
    nk = MLA_HEADS * MLA_NOPE
    n_stages = DEC_CHAINS + DEC_SKEW

    def page_copy(a, slot, k, page):
        return pltpu.make_async_copy(hbm[a].at[layer, page], bufs[a].at[slot, k], sems.at[a, slot])

    def start_step(t, slot):
        b = t // n_steps
        first = (t - b * n_steps) * pp
        for k in range(pp):
            page = pt_ref[b, first + k]
            for a in range(len(hbm)):
                page_copy(a, slot, k, page).start()

    def wait_step(slot):
        for k in range(pp):
            for a in range(len(hbm)):
                page_copy(a, slot, k, 0).wait()

    def update(s, m_ref, l_ref, c):
        m_prev = m_ref[c]
        m_new = jnp.maximum(m_prev, jnp.max(s, axis=-1, keepdims=True))
        alpha = jnp.exp2(m_prev - m_new)
        p = jnp.exp2(s - m_new)
        l_ref[c] = alpha * l_ref[c] + jnp.sum(p, axis=-1, keepdims=True)
        m_ref[c] = m_new
        return alpha, p

    def pages(c):
        return range(c * DEC_CHAIN_PAGES, (c + 1) * DEC_CHAIN_PAGES)

    def stages(t):
        slot = t & 1
        b = t // n_steps
        j = t - b * n_steps
        row = lax.broadcasted_iota(jnp.int32, (MLA_HEADS, 1), 0)
        row_g0 = ((row >> 1) & 1) == 0
        prods = {}

        def begin():
            start_step(jnp.minimum(t + 1, total - 1), 1 - slot)
            wait_step(slot)

            @pl.when(j == 0)
            def _():
                lhs_ref[nk:, :] = jnp.concatenate(
                    [qa_ref[b].astype(BF16), jnp.zeros((lhs_ref.shape[0] - nk - MLA_HEADS, LANES), BF16)], axis=0)
                for r in (m_a, m_d):
                    r[...] = jnp.full(r.shape, NEG, F32)
                for r in (l_a, acc_a, l_d, acc_d):
                    r[...] = jnp.zeros(r.shape, F32)

        def score_products(c):
            ckv = jnp.concatenate([ckv_buf[slot, k] for k in pages(c)], axis=0).astype(BF16)
            res = _dot_nt(lhs_ref[...], ckv)
            kpe_t = jnp.concatenate([kpe_buf[slot, k] for k in pages(c)], axis=1).astype(BF16)
            bp = _dot(qp_ref[b], kpe_t)
            kt = jnp.concatenate([kt_buf[slot, k] for k in pages(c)], axis=1).astype(BF16)
            return ckv, res, bp, _dot(qbd_ref[b], kt)

        def softmax(c, res, bp, sd):
            sq = res[:nk] * res[:nk]
            ssq = jnp.concatenate(
                [jnp.sum(sq[h * MLA_NOPE:(h + 1) * MLA_NOPE], axis=0, keepdims=True) for h in range(MLA_HEADS)],
                axis=0)
            rnorm = lax.rsqrt(ssq * (1.0 / MLA_NOPE) + EPS)
            alpha_a, p_a = update(res[nk:nk + MLA_HEADS] * rnorm + bp, m_a, l_a, c)
            alpha_d, p_d = update(sd, m_d, l_d, c)
            return alpha_a, p_a.astype(BF16), alpha_d, p_d.astype(BF16)

        def value_products(c, ckv, alpha_a, p_a, alpha_d, p_d):
            acc_a[c] = alpha_a * acc_a[c] + _dot(p_a, ckv)
            pv = []
            for g in range(DIFF_KV_HEADS):
                v = jnp.concatenate(
                    [v_buf[slot, k, pl.ds(g, PAGE_SIZE, stride=DIFF_KV_HEADS), :] for k in pages(c)], axis=0)
                pv.append(_dot(p_d, v.astype(BF16)))
            acc_d[c] = alpha_d * acc_d[c] + jnp.where(row_g0, pv[0], pv[1])

        def merged(s_self, v_self, m_ref, l_ref, acc_ref):
            m = s_self
            for c in range(DEC_CHAINS):
                m = jnp.maximum(m, m_ref[c])
            p_self = jnp.exp2(s_self - m)
            l = p_self
            acc = p_self * v_self
            for c in range(DEC_CHAINS):
                w = jnp.exp2(m_ref[c] - m)
                l = l + w * l_ref[c]
                acc = acc + w * acc_ref[c]
            return acc / l

        def finish():
            @pl.when(j == n_steps - 1)
            def _():
                s_self = jnp.sum(qf8_ref[b].astype(F32) * kf8_ref[b].astype(F32), axis=-1, keepdims=True)
                lat = merged(s_self, ckvrow_ref[b], m_a, l_a, acc_a)
                full = _dot(lat.astype(BF16), wuv_ref[...])
                col_head = lax.broadcasted_iota(jnp.int32, full.shape, 1) >> 6
                row_head = lax.broadcasted_iota(jnp.int32, full.shape, 0)
                oa_ref[b] = jnp.sum(jnp.where(col_head == row_head, full, 0.0), axis=0, keepdims=True)
                s_self = jnp.sum(qbd_ref[b].astype(F32) * dkrow_ref[b].astype(F32), axis=-1, keepdims=True)
                dvrow = dvrow_ref[b]
                v_self = jnp.where(row_g0, dvrow[:, :DIFF_VD], dvrow[:, DIFF_VD:])
                o = merged(s_self, v_self, m_d, l_d, acc_d)
                lam = _lam(lq1_ref, lk1_ref, lq2_ref, lk2_ref, lam_init)
                d = o[:DIFF_HEADS] - lam * o[DIFF_HEADS:]
                ob_ref[b] = _rms(d, gsub_ref[...]) * (1.0 - lam_init)

        def make(step):
            def run():
                if step == 0:
                    begin()
                if step < DEC_CHAINS:
                    prods[step] = score_products(step)
                c = step - DEC_SKEW
                if c >= 0:
                    ckv, res, bp, sd = prods.pop(c)
                    value_products(c, ckv, *softmax(c, res, bp, sd))
                if step == n_stages - 1:
                    finish()
            return run

        return [make(step) for step in range(n_stages)]

    def prime():
        lhs_ref[:nk, :] = wukt_ref[...]
        start_step(0, 0)

    def drain():
        wait_step(total & 1)

    return prime, drain, stages, total


def _dec_scratch(caches):
    return ([pltpu.VMEM((2, DEC_PAGES) + c.shape[2:], c.dtype) for c in caches]
            + [pltpu.SemaphoreType.DMA((len(caches), 2)),
               pltpu.VMEM((MLA_HEADS * MLA_NOPE + 16, LANES), BF16),
               pltpu.VMEM((DEC_CHAINS, MLA_HEADS, 1), F32), pltpu.VMEM((DEC_CHAINS, MLA_HEADS, 1), F32),
               pltpu.VMEM((DEC_CHAINS, MLA_HEADS, MLA_KV_RANK), F32),
               pltpu.VMEM((DEC_CHAINS, MLA_HEADS, 1), F32), pltpu.VMEM((DEC_CHAINS, MLA_HEADS, 1), F32),
               pltpu.VMEM((DEC_CHAINS, MLA_HEADS, DIFF_VD), F32)])


def _dec_kernel(*refs, layer, n_seq, n_pages, lam_init):
    prime, drain, stages, total = _dec_pipeline(
        refs, layer=layer, n_seq=n_seq, n_pages=n_pages, lam_init=lam_init)

    def body(t, carry):
        for run in stages(t):
            run()
        return carry

    prime()
    lax.fori_loop(0, total, body, 0)
    drain()


def _decode(page_table, per_seq, consts, caches, layer, lam_init):
    b, n_pages = page_table.shape

    def full_spec(a):
        return pl.BlockSpec(a.shape, lambda i, pt: (0,) * a.ndim)

    oa_w = MLA_HEADS * MLA_V
    out_shape = [jax.ShapeDtypeStruct((b, 1, oa_w), F32), jax.ShapeDtypeStruct((b, DIFF_HEADS, DIFF_VD), F32)]
    grid_spec = pltpu.PrefetchScalarGridSpec(
        num_scalar_prefetch=1,
        grid=(1,),
        in_specs=[full_spec(a) for a in list(per_seq) + list(consts)]
                 + [pl.BlockSpec(memory_space=pl.ANY)] * len(caches),
        out_specs=[full_spec(o) for o in out_shape],
        scratch_shapes=_dec_scratch(caches),
    )
    return pl.pallas_call(
        functools.partial(_dec_kernel, layer=layer, n_seq=b, n_pages=n_pages, lam_init=lam_init),
        grid_spec=grid_spec,
        out_shape=out_shape,
        compiler_params=_cparams(("arbitrary",)),
        name="decode",
    )(page_table, *per_seq, *consts, *caches)


def _out_kernel(mix_ref, x_ref, gt_ref, sc_ref, sh_ref, wo_ref, g2_ref, wrh_ref, wrl_ref, br_ref,
                x1_ref, h2_ref, gate_ref):
    o = _dot(mix_ref[...], wo_ref[...])
    x1 = x_ref[...] + gt_ref[...] * o
    x1_ref[...] = x1
    h2 = _rms(x1, g2_ref[...]) * (1.0 + sc_ref[...]) + sh_ref[...]
    h2_ref[...] = h2.astype(BF16)
    hh, hl = _split(h2)
    logits = _dot(hh, wrh_ref[...]) + _dot(hh, wrl_ref[...]) + _dot(hl, wrh_ref[...]) + br_ref[...]
    lane_i = lax.broadcasted_iota(jnp.int32, logits.shape, 1)
    lane = lane_i.astype(F32)
    big = float(ROUTER_LANES)
    gl = jnp.where(lane_i < N_GROUPS, logits, NEG)
    gmax = jnp.max(gl, axis=-1, keepdims=True)
    gidx = jnp.min(jnp.where(gl == gmax, lane, big), axis=-1, keepdims=True)
    g_w = 1.0 / jnp.sum(jnp.exp(gl - gmax), axis=-1, keepdims=True)
    in_group = (lane_i >= N_GROUPS) & (lane_i < N_GROUPS + N_ROUTED) & (
        ((lane_i - N_GROUPS) >> 3).astype(F32) == gidx)
    el = jnp.where(in_group, logits, NEG)
    e1 = jnp.max(el, axis=-1, keepdims=True)
    i1 = jnp.min(jnp.where(el == e1, lane, big), axis=-1, keepdims=True)
    el2 = jnp.where(lane == i1, NEG, el)
    e2 = jnp.max(el2, axis=-1, keepdims=True)
    i2 = jnp.min(jnp.where(el2 == e2, lane, big), axis=-1, keepdims=True)
    t = jnp.exp(e2 - e1)
    w1 = 1.0 / (1.0 + t)
    w2 = t / (1.0 + t)
    gate_ref[...] = jnp.where(lane == i1, w1, jnp.where(lane == i2, w2, 0.0)) * g_w


def _out(mix, x3, gt, sc, sh, wo, g2, wrh, wrl, br, tm):
    b, t, d = x3.shape
    per_tok = gt.shape[1] != 1

    def tok_spec(w):
        return pl.BlockSpec((None, tm, w), lambda s, bb: (bb, s, 0))

    mod_spec = tok_spec(d) if per_tok else pl.BlockSpec((None, 1, d), lambda s, bb: (bb, 0, 0))
    consts = [wo, g2, wrh, wrl, br]
    return pl.pallas_call(
        _out_kernel,
        grid=(t // tm, b),
        in_specs=[tok_spec(mix.shape[2]), tok_spec(d), mod_spec, mod_spec, mod_spec] + [_const_spec(c.shape) for c in consts],
        out_specs=[tok_spec(d), tok_spec(d), tok_spec(ROUTER_LANES)],
        out_shape=[jax.ShapeDtypeStruct((b, t, d), F32), jax.ShapeDtypeStruct((b, t, d), BF16),
                   jax.ShapeDtypeStruct((b, t, ROUTER_LANES), F32)],
        compiler_params=_cparams(("parallel", "parallel")),
        name="out",
    )(mix, x3, gt, sc, sh, *consts)


MOE_CHUNK = MXU_DIM


def _moe_kernel(h2_ref, gate_ref, x1_ref, gt_ref, wg_ref, wu_ref, wd_ref, ex_ref, y_ref):
    g = pl.program_id(2)

    @pl.when(g == 0)
    def _():
        y_ref[...] = jnp.zeros(y_ref.shape, F32)

    h = h2_ref[...]
    gate_b = gate_ref[...].astype(BF16)
    for c in range(GROUP_FF // MOE_CHUNK):
        cs = slice(c * MOE_CHUNK, (c + 1) * MOE_CHUNK)
        ge = _dot(gate_b, ex_ref[:, cs])
        a = _silu(_dot(h, wg_ref[:, cs])) * _dot(h, wu_ref[:, cs]) * ge
        y_ref[...] += _dot(a.astype(BF16), wd_ref[cs, :])

    @pl.when(g == N_GROUPS - 1)
    def _():
        y_ref[...] = x1_ref[...] + gt_ref[...] * y_ref[...]


def _moe(h2, gate, x1, gt, wg, wu, wd, ex, tm):
    b, t, d = x1.shape
    per_tok = gt.shape[1] != 1

    def tok_spec(w):
        return pl.BlockSpec((None, tm, w), lambda s, bb, g: (bb, s, 0))

    mod_spec = tok_spec(d) if per_tok else pl.BlockSpec((None, 1, d), lambda s, bb, g: (bb, 0, 0))

    def grp_spec(a):
        return pl.BlockSpec((None,) + a.shape[1:], lambda s, bb, g: (g, 0, 0))

    return pl.pallas_call(
        _moe_kernel,
        grid=(t // tm, b, N_GROUPS),
        in_specs=[tok_spec(d), tok_spec(ROUTER_LANES), tok_spec(d), mod_spec,
                  grp_spec(wg), grp_spec(wu), grp_spec(wd), grp_spec(ex)],
        out_specs=tok_spec(d),
        out_shape=jax.ShapeDtypeStruct((b, t, d), F32),
        compiler_params=_cparams(("parallel", "parallel", "arbitrary")),
        name="moe",
    )(h2, gate, x1, gt, wg, wu, wd, ex)


def _token_tiles(seq):
    return min(512, seq), min(512, seq), min(512, seq), min(1024, seq)


def _rope_tables(pos):
    def cs(dim):
        half = dim // 2
        inv = ROPE_THETA ** (-jnp.arange(half, dtype=F32) * 2.0 / dim)
        ang = pos[:, None] * inv[None, :]
        c, s = jnp.cos(ang), jnp.sin(ang)
        return jnp.concatenate([c, c], axis=1), jnp.concatenate([-s, s], axis=1)

    t = pos.shape[0]
    c32, s32 = cs(MLA_ROPE)
    c64, s64 = cs(DIFF_HD)
    z = lambda w: jnp.zeros((t, w), F32)
    pad = LANES - MLA_NOPE - MLA_ROPE
    cosq = jnp.concatenate([jnp.ones((t, MLA_NOPE), F32), c32, z(pad)], axis=1)
    sinq = jnp.concatenate([z(MLA_NOPE), s32, z(pad)], axis=1)
    cosk = jnp.concatenate([c32, z(LANES - MLA_ROPE)], axis=1)
    sink = jnp.concatenate([s32, z(LANES - MLA_ROPE)], axis=1)
    cosd = jnp.concatenate([c64, c64], axis=1)
    sind = jnp.concatenate([s64, s64], axis=1)
    return [cosq, sinq, cosk, sink, cosd, sind]


def _block_diag_mean(sizes, width):
    m = jnp.zeros((width, width), F32)
    o = 0
    while o < width:
        for sz in sizes:
            if sz > 0:
                m = m.at[o:o + sz, o:o + sz].set(1.0 / sz)
            o += abs(sz)
    return m.astype(BF16)


def _layer_weights(l, w_in, g_norm1, g_mla_qa, w_mla_uq, g_mla_kva, w_mla_uk, g_mla_qn_nope, g_mla_qn_rope,
                   g_mla_kn_nope, g_mla_kn_rope, g_diff_qn, g_diff_kn):
    d = w_in.shape[1]
    o_kpe = MLA_Q_RANK + MLA_KV_RANK
    wi = w_in[l]
    win = jnp.concatenate([wi[:, :o_kpe], wi[:, o_kpe:o_kpe + MLA_ROPE], jnp.zeros((d, LANES - MLA_ROPE), F32),
                           wi[:, o_kpe + MLA_ROPE:]], axis=1).astype(BF16)
    pad = LANES - MLA_NOPE - MLA_ROPE
    wuq = w_mla_uq[l].reshape(MLA_Q_RANK, MLA_HEADS, MLA_NOPE + MLA_ROPE)
    wuq = jnp.concatenate([wuq, jnp.zeros((MLA_Q_RANK, MLA_HEADS, pad), F32)], axis=2).reshape(MLA_Q_RANK, QK_W).astype(BF16)
    wuk = jnp.concatenate([w_mla_uk[l], jnp.zeros((MLA_KV_RANK, MLA_HEADS, LANES - MLA_NOPE), F32)], axis=2)
    wuk = wuk.reshape(MLA_KV_RANK, QK_W).astype(BF16)
    gq = jnp.tile(jnp.concatenate([g_mla_qn_nope[l], g_mla_qn_rope[l], jnp.zeros((pad,), F32)]), MLA_HEADS)[None]
    gk = jnp.tile(jnp.concatenate([g_mla_kn_nope[l], jnp.zeros((LANES - MLA_NOPE,), F32)]), MLA_HEADS)[None]
    gkpe = jnp.concatenate([g_mla_kn_rope[l], jnp.zeros((LANES - MLA_ROPE,), F32)])[None]
    gdq = jnp.tile(g_diff_qn[l], DQ_W // DIFF_HD)[None]
    gdk = jnp.tile(g_diff_kn[l], DK_W // DIFF_HD)[None]
    bdq = _block_diag_mean((MLA_NOPE, MLA_ROPE, -pad), MXU_DIM)
    bdd = _block_diag_mean((DIFF_HD,), MXU_DIM)
    return [g_norm1[l][None], win, g_mla_qa[l][None], wuq, gq, g_mla_kva[l][None], wuk, gk, gkpe, gdq, gdk, bdq, bdd]


def kernel(x_prompt, x_sample, cache_mla_ckv, cache_mla_kpe, cache_diff_k, cache_diff_v, page_table, c_prompt, c_sample, w_ada, b_ada, g_norm1, w_in, g_mla_qa, w_mla_uq, g_mla_kva, w_mla_uk, w_mla_uv, g_mla_qn_nope, g_mla_qn_rope, g_mla_kn_nope, g_mla_kn_rope, g_diff_qn, g_diff_kn, lam_q1, lam_k1, lam_q2, lam_k2, g_diff_subln, w_o, g_norm2, w_router_group, b_router_group, w_router_expert, b_router_expert, w_exp_gate, w_exp_up, w_exp_down):
    bp, sp, d = x_prompt.shape
    bs, ts, _ = x_sample.shape
    depth = w_in.shape[0]
    n_pool = cache_mla_ckv.shape[1]
    n_pages = page_table.shape[1]
    assert ts == 1 and n_pages % DEC_PAGES == 0 and cache_mla_ckv.shape[2] == PAGE_SIZE
    assert w_in.shape[2] == PROJ_W - LANES + MLA_ROPE and d % MXU_DIM == 0
    past = n_pages * PAGE_SIZE

    tm_p, tq, tm_o, tm_m = _token_tiles(sp)
    tables_p = _rope_tables(jnp.arange(sp, dtype=F32))
    tables_s = _rope_tables(jnp.arange(ts, dtype=F32) + past)

    kpe_t = jnp.transpose(cache_mla_kpe, (0, 1, 3, 2))
    k_t = jnp.transpose(cache_diff_k, (0, 1, 3, 4, 5, 2)).reshape(depth, n_pool, DK_W, PAGE_SIZE)
    v_rows = cache_diff_v.reshape(depth, n_pool, PAGE_SIZE * DIFF_KV_HEADS, DIFF_VD)

    xp = x_prompt
    xs = x_sample.reshape(1, bs, d)
    outs_p = [[], [], [], []]
    outs_s = [[], [], [], []]
    for l in range(depth):
        lam_init = 0.8 - 0.6 * math.exp(-0.3 * l)
        wts = _layer_weights(l, w_in, g_norm1, g_mla_qa, w_mla_uq, g_mla_kva, w_mla_uk, g_mla_qn_nope,
                             g_mla_qn_rope, g_mla_kn_nope, g_mla_kn_rope, g_diff_qn, g_diff_kn)
        lams = [lam_q1[l][None], lam_k1[l][None], lam_q2[l][None], lam_k2[l][None]]
        gsub = g_diff_subln[l][None]
        wuv = w_mla_uv[l]
        wuv_flat = wuv.reshape(MLA_KV_RANK, MLA_HEADS * MLA_V).astype(BF16)
        wuvt = jnp.transpose(wuv, (1, 2, 0)).astype(BF16)
        wuk_t = jnp.concatenate([jnp.transpose(w_mla_uk[l], (1, 2, 0)),
                                 jnp.zeros((MLA_HEADS, LANES - MLA_NOPE, MLA_KV_RANK), F32)], axis=1).astype(BF16)
        wukt_rows = jnp.transpose(w_mla_uk[l], (1, 2, 0)).reshape(MLA_HEADS * MLA_NOPE, MLA_KV_RANK).astype(BF16)
        gk_row = jnp.concatenate([g_mla_kn_nope[l], jnp.zeros((LANES - MLA_NOPE,), F32)])[None]
        wo = w_o[l].astype(BF16)
        wr = jnp.concatenate([w_router_group[l], jnp.transpose(w_router_expert[l], (1, 0, 2)).reshape(d, N_ROUTED),
                              jnp.zeros((d, ROUTER_LANES - N_GROUPS - N_ROUTED), F32)], axis=1)
        wrh, wrl = _split(wr)
        br = jnp.concatenate([b_router_group[l], b_router_expert[l].reshape(N_ROUTED),
                              jnp.zeros((ROUTER_LANES - N_GROUPS - N_ROUTED,), F32)])[None]
        wg = jnp.transpose(w_exp_gate[l], (0, 2, 1, 3)).reshape(N_GROUPS, d, GROUP_FF).astype(BF16)
        wu = jnp.transpose(w_exp_up[l], (0, 2, 1, 3)).reshape(N_GROUPS, d, GROUP_FF).astype(BF16)
        wd = w_exp_down[l].reshape(N_GROUPS, GROUP_FF, d).astype(BF16)
        lane = jnp.arange(ROUTER_LANES)[None, :, None]
        col = jnp.arange(GROUP_FF)[None, None, :]
        grp = jnp.arange(N_GROUPS)[:, None, None]
        ex = (lane == N_GROUPS + grp * EXPERTS_PER_GROUP + col // EXPERT_FF).astype(BF16)

        mod = _ada(jnp.concatenate([c_prompt, c_sample], axis=0), w_ada[l], b_ada[l][None])
        mod_p = mod[:bp].reshape(bp, 6, 1, d)
        mod_s = mod[bp:].reshape(1, bs, 6, d)
        sh1p, sc1p, gt1p, sh2p, sc2p, gt2p = [mod_p[:, k] for k in range(6)]
        sh1s, sc1s, gt1s, sh2s, sc2s, gt2s = [mod_s[:, :, k] for k in range(6)]

        (qf, kf, dq0, dq1, ckvt, dkb, dvt, ckv, kpe, dk, dv) = _proj(xp, sc1p, sh1p, tables_p, wts, tm_p)
        mix = _attn(qf, dq0, dq1, kf, ckvt, dkb, dvt, wuvt, gsub.reshape(DIFF_VD, 1), lams, lam_init, tq)
        x1, h2, gate = _out(mix, xp, gt1p, sc2p, sh2p, wo, g_norm2[l][None], wrh, wrl, br, tm_o)
        for lst, a in zip(outs_p, (ckv, kpe, dk, dv)):
            lst.append(a)
        (qf, kf, dq0, dq1, _, dkb, _, ckv, kpe, dk, dv) = _proj(xs, sc1s, sh1s, tables_s, wts, bs)
        for lst, a in zip(outs_s, (ckv, kpe, dk, dv)):
            lst.append(a)
        qf2 = qf.reshape(bs, QK_W)
        qa = _qabs(qf2, gk_row, wuk_t).reshape(bs, MLA_HEADS, LANES)
        qf8 = qf2.reshape(bs, MLA_HEADS, LANES)
        kf8 = kf.reshape(bs, MLA_HEADS, LANES)
        qp = qf8[:, :, MLA_NOPE:MLA_NOPE + MLA_ROPE]
        dq5 = (dq0 + dq1).reshape(bs, DIFF_KV_HEADS, 2, 2, DIFF_HD)
        eye = jnp.eye(2, dtype=BF16)
        qbd = jnp.einsum('bgrmd,gh,mn->bmgrhnd', dq5, eye, eye).reshape(bs, 2 * DIFF_HEADS, DK_W)
        per_seq = [qa, qp, qbd, qf8, kf8, dkb.reshape(bs, 1, DK_W), ckv.reshape(bs, 1, MLA_KV_RANK),
                   dv.reshape(bs, 1, DV_W)]
        consts = [wukt_rows, wuv_flat, gsub] + lams
        caches = (cache_mla_ckv, kpe_t, k_t, v_rows)

        xp = _moe(h2, gate, x1, gt2p, wg, wu, wd, ex, tm_m)
        oa, ob = _decode(page_table, per_seq, consts, caches, l, lam_init)
        mix = jnp.concatenate([oa.reshape(bs, -1), ob.reshape(bs, -1)], axis=1).astype(BF16).reshape(1, bs, -1)
        x1, h2, gate = _out(mix, xs, gt1s, sc2s, sh2s, wo, g_norm2[l][None], wrh, wrl, br, bs)
        xs = _moe(h2, gate, x1, gt2s, wg, wu, wd, ex, bs)

    def stack_p(lst, tail):
        return jnp.stack(lst).reshape((depth, bp, sp) + tail)

    def stack_s(lst, tail):
        return jnp.stack(lst).reshape((depth, bs, ts) + tail)

    k_tail = (DIFF_KV_HEADS, 2, DIFF_HD)
    v_tail = (DIFF_KV_HEADS, DIFF_VD)
    return (xp, xs.reshape(bs, ts, d),
            stack_p(outs_p[0], (MLA_KV_RANK,)), stack_p(outs_p[1], (MLA_ROPE,)), stack_p(outs_p[2], k_tail),
            stack_p(outs_p[3], v_tail),
            stack_s(outs_s[0], (MLA_KV_RANK,)), stack_s(outs_s[1], (MLA_ROPE,)), stack_s(outs_s[2], k_tail),
            stack_s(outs_s[3], v_tail))
```

```python
import functools
import math

import jax
import jax.numpy as jnp
from jax import lax
from jax.experimental import pallas as pl
from jax.experimental.pallas import tpu as pltpu

F32 = jnp.float32
BF16 = jnp.bfloat16

MLA_HEADS = 8
MLA_Q_RANK = 256
MLA_KV_RANK = 128
MLA_NOPE = 64
MLA_ROPE = 32
MLA_V = 64
DIFF_HEADS = 4
DIFF_KV_HEADS = 2
DIFF_HD = 64
DIFF_VD = 128
N_GROUPS = 4
EXPERTS_PER_GROUP = 8
EXPERT_FF = 128
PAGE_SIZE = 128
ROPE_THETA = 10000.0
EPS = 1e-6
LOG2E = 1.4426950408889634
MLA_SCALE = (MLA_NOPE + MLA_ROPE) ** -0.5
DIFF_SCALE = DIFF_HD ** -0.5
NEG = -1e30

LANES = 128
MXU_DIM = 256
VMEM_LIMIT = 56 * 1024 * 1024
ROUTER_LANES = 128
N_ROUTED = N_GROUPS * EXPERTS_PER_GROUP
GROUP_FF = EXPERTS_PER_GROUP * EXPERT_FF
QK_W = MLA_HEADS * LANES
DQ_W = DIFF_HEADS * 2 * DIFF_HD
DK_W = DIFF_KV_HEADS * 2 * DIFF_HD
DV_W = DIFF_KV_HEADS * DIFF_VD
PROJ_W = MLA_Q_RANK + MLA_KV_RANK + LANES + DQ_W + DK_W + DV_W
DEC_PAGES = 32
DEC_CHAIN_PAGES = 2
DEC_CHAINS = DEC_PAGES // DEC_CHAIN_PAGES
DEC_SKEW = 3


def _dot(a, b):
    return jnp.dot(a, b, preferred_element_type=F32)


def _dot_nt(a, b):
    return lax.dot_general(a, b, (((1,), (1,)), ((), ())), preferred_element_type=F32)


def _split(a):
    hi = a.astype(BF16)
    lo = (a - hi.astype(F32)).astype(BF16)
    return hi, lo


def _rms(v, g):
    return v * lax.rsqrt(jnp.mean(v * v, axis=-1, keepdims=True) + EPS) * g


def _silu(v):
    return v / (1.0 + jnp.exp(-v))


def _cparams(sem):
    return pltpu.CompilerParams(dimension_semantics=sem, vmem_limit_bytes=VMEM_LIMIT)


def _const_spec(shape):
    nd = len(shape)
    return pl.BlockSpec(shape, lambda *_: (0,) * nd)


def _ada_kernel(c_ref, w_ref, b_ref, o_ref):
    s = _silu(c_ref[...])
    sh, sl = _split(s)
    wh, wl = _split(w_ref[...])
    o_ref[...] = _dot(sh, wh) + _dot(sh, wl) + _dot(sl, wh) + b_ref[...]


def _ada(c, w, b):
    m, d = c.shape
    n = w.shape[1]
    tn = 512
    return pl.pallas_call(
        _ada_kernel,
        grid=(n // tn,),
        in_specs=[_const_spec((m, d)), pl.BlockSpec((d, tn), lambda i: (0, i)), pl.BlockSpec((1, tn), lambda i: (0, i))],
        out_specs=pl.BlockSpec((m, tn), lambda i: (0, i)),
        out_shape=jax.ShapeDtypeStruct((m, n), F32),
        compiler_params=_cparams(("parallel",)),
        name="ada",
    )(c, w, b)


def _block_norm(v, bd, g):
    w = v.shape[1]
    sq = (v * v).astype(BF16)
    ms = jnp.concatenate([_dot(sq[:, i:i + MXU_DIM], bd) for i in range(0, w, MXU_DIM)], axis=1)
    return v * lax.rsqrt(ms + EPS) * g


def _rope(v, cos, sin, half, first):
    parts = []
    for i in range(0, v.shape[1], LANES):
        s = v[:, i:i + LANES]
        rot = jnp.where(first, pltpu.roll(s, LANES - half, 1), pltpu.roll(s, half, 1))
        parts.append(s * cos + rot * sin)
    return parts[0] if len(parts) == 1 else jnp.concatenate(parts, axis=1)


def _proj_kernel(x_ref, sc_ref, sh_ref, cosq_ref, sinq_ref, cosk_ref, sink_ref, cosd_ref, sind_ref,
                 g1_ref, win_ref, gqa_ref, wuq_ref, gq_ref, gkva_ref, wuk_ref, gk_ref, gkpe_ref, gdq_ref, gdk_ref,
                 bdq_ref, bdd_ref,
                 qf_ref, kf_ref, dq0_ref, dq1_ref, ckvt_ref, dkb_ref, dvt_ref, ckv_ref, kpe_ref, dk_ref, dv_ref):
    lane = lax.broadcasted_iota(jnp.int32, (1, LANES), 1)
    x = x_ref[...]
    h = _rms(x, g1_ref[...]) * (1.0 + sc_ref[...]) + sh_ref[...]
    proj = _dot(h.astype(BF16), win_ref[...])
    o_ckv = MLA_Q_RANK
    o_kpe = o_ckv + MLA_KV_RANK
    o_dq = o_kpe + LANES
    o_dk = o_dq + DQ_W
    o_dv = o_dk + DK_W

    cqn = _rms(proj[:, :MLA_Q_RANK], gqa_ref[...])
    q = _dot(cqn.astype(BF16), wuq_ref[...])
    qn = _block_norm(q, bdq_ref[...], gq_ref[...])
    q_first = (lane >= MLA_NOPE) & (lane < MLA_NOPE + MLA_ROPE // 2)
    qf = _rope(qn, cosq_ref[...], sinq_ref[...], MLA_ROPE // 2, q_first)
    qf_ref[...] = (qf * (MLA_SCALE * LOG2E)).astype(BF16)

    ckv = _rms(proj[:, o_ckv:o_kpe], gkva_ref[...])
    ckv_ref[...] = ckv
    ckvb = ckv.astype(BF16)
    ckvt_ref[...] = ckv.T.astype(BF16)
    kr = proj[:, o_kpe:o_dq]
    kn = kr * lax.rsqrt(jnp.sum(kr * kr, axis=-1, keepdims=True) * (1.0 / MLA_ROPE) + EPS) * gkpe_ref[...]
    kpe = _rope(kn, cosk_ref[...], sink_ref[...], MLA_ROPE // 2, lane < MLA_ROPE // 2)
    kpe_ref[...] = kpe[:, :MLA_ROPE]

    kraw = _dot(ckvb, wuk_ref[...])
    knn = _block_norm(kraw, bdq_ref[...], gk_ref[...])
    kpe_at_rope = pltpu.roll(kpe, MLA_NOPE, 1)
    kf_ref[...] = jnp.concatenate(
        [knn[:, i:i + LANES] + kpe_at_rope for i in range(0, QK_W, LANES)], axis=1).astype(BF16)

    d_first = (lane & (DIFF_HD - 1)) < DIFF_HD // 2
    dq = _block_norm(proj[:, o_dq:o_dk], bdd_ref[...], gdq_ref[...])
    dq = _rope(dq, cosd_ref[...], sind_ref[...], DIFF_HD // 2, d_first) * (DIFF_SCALE * LOG2E)
    map0 = (lax.broadcasted_iota(jnp.int32, (1, DQ_W), 1) & (LANES - 1)) < DIFF_HD
    dq0_ref[...] = jnp.where(map0, dq, 0.0).astype(BF16)
    dq1_ref[...] = jnp.where(map0, 0.0, dq).astype(BF16)
    dk = _block_norm(proj[:, o_dk:o_dv], bdd_ref[...], gdk_ref[...])
    dk = _rope(dk, cosd_ref[...], sind_ref[...], DIFF_HD // 2, d_first)
    dk_ref[...] = dk
    dkb_ref[...] = dk.astype(BF16)
    dv = proj[:, o_dv:]
    dv_ref[...] = dv
    dvt_ref[...] = dv.T.astype(BF16)


def _proj(x3, sc, sh, tables, wts, tm):
    b, t, d = x3.shape
    nt = t // tm
    per_tok = sc.shape[1] != 1
    tab_rows = tables[0].shape[0]

    def tok_spec(w):
        return pl.BlockSpec((None, tm, w), lambda s, bb: (bb, s, 0))

    mod_spec = tok_spec(d) if per_tok else pl.BlockSpec((None, 1, d), lambda s, bb: (bb, 0, 0))
    tab_spec = (pl.BlockSpec((tm, LANES), lambda s, bb: (s, 0)) if tab_rows != 1
                else pl.BlockSpec((1, LANES), lambda s, bb: (0, 0)))
    in_specs = [tok_spec(d), mod_spec, mod_spec] + [tab_spec] * 6 + [_const_spec(w.shape) for w in wts]
    outs = [(QK_W, BF16, False), (QK_W, BF16, False), (DQ_W, BF16, False), (DQ_W, BF16, False),
            (MLA_KV_RANK, BF16, True), (DK_W, BF16, False), (DV_W, BF16, True),
            (MLA_KV_RANK, F32, False), (MLA_ROPE, F32, False), (DK_W, F32, False), (DV_W, F32, False)]

    def out_spec(w, tr):
        return pl.BlockSpec((None, w, tm), lambda s, bb: (bb, 0, s)) if tr else tok_spec(w)

    return pl.pallas_call(
        _proj_kernel,
        grid=(nt, b),
        in_specs=in_specs,
        out_specs=[out_spec(w, tr) for w, _, tr in outs],
        out_shape=[jax.ShapeDtypeStruct((b, w, t) if tr else (b, t, w), dt) for w, dt, tr in outs],
        compiler_params=_cparams(("parallel", "parallel")),
        name="proj",
    )(x3, sc, sh, *tables, *wts)


def _lam(lq1_ref, lk1_ref, lq2_ref, lk2_ref, lam_init):
    a = jnp.sum(lq1_ref[...] * lk1_ref[...], axis=-1, keepdims=True)
    b = jnp.sum(lq2_ref[...] * lk2_ref[...], axis=-1, keepdims=True)
    return jnp.exp(a) - jnp.exp(b) + lam_init


ATTN_MAPS = MLA_HEADS + 2 * DIFF_HEADS
ATTN_SLOTS = 2


def _col_tree(x, op):
    parts = [x[c * 64:(c + 1) * 64] for c in range(x.shape[0] // 64)]
    while len(parts) > 1:
        parts = [op(parts[a], parts[a + 1]) for a in range(0, len(parts), 2)]
    return parts[0]


def _attn_kernel(qf_ref, dq0_ref, dq1_ref, kf_ref, ckvt_ref, dk_ref, dvt_ref, wuvt_ref, gsub_ref,
                 lq1_ref, lk1_ref, lq2_ref, lk2_ref, out_ref, m_s, l_s, acc_s, s_scr, *, tq, lam_init):
    i = pl.program_id(1)
    j = pl.program_id(2)

    @pl.when(j == 0)
    def _():
        m_s[...] = jnp.full(m_s.shape, NEG, F32)
        l_s[...] = jnp.zeros(l_s.shape, F32)
        acc_s[...] = jnp.zeros(acc_s.shape, F32)

    maps = []
    for h in range(MLA_HEADS):
        sl = slice(h * LANES, (h + 1) * LANES)
        maps.append((qf_ref, sl, kf_ref, sl, None, h))
    for g in range(DIFF_KV_HEADS):
        gs = slice(g * LANES, (g + 1) * LANES)
        for r in range(DIFF_HEADS // DIFF_KV_HEADS):
            sl = slice((g * 2 + r) * LANES, (g * 2 + r + 1) * LANES)
            maps.append((dq0_ref, sl, dk_ref, gs, gs, MLA_HEADS + (g * 2 + r) * 2))
            maps.append((dq1_ref, sl, dk_ref, gs, gs, MLA_HEADS + (g * 2 + r) * 2 + 1))

    def scores(n, pieces, masked):
        q_ref, qs, k_ref, ks, _, _ = maps[n]
        for q0, q1, nk in pieces:
            st = _dot_nt(k_ref[:nk, ks], q_ref[q0:q1, qs])
            if masked:
                key = lax.broadcasted_iota(jnp.int32, st.shape, 0)
                qry = lax.broadcasted_iota(jnp.int32, st.shape, 1) + q0
                st = jnp.where(key <= qry, st, NEG)
            s_scr[n % ATTN_SLOTS, :nk, q0:q1] = st

    def softmax_pv(n, pieces):
        _, _, _, _, vs, idx = maps[n]
        for q0, q1, nk in pieces:
            vt = ckvt_ref[:, :nk] if vs is None else dvt_ref[vs, :nk]
            st = s_scr[n % ATTN_SLOTS, :nk, q0:q1]
            m_prev = m_s[idx, :, q0:q1]
            m_new = jnp.maximum(m_prev, jnp.max(_col_tree(st, jnp.maximum), axis=0, keepdims=True))
            alpha = jnp.exp2(m_prev - m_new)
            p = jnp.exp2(st - m_new)
            l_s[idx, :, q0:q1] = alpha * l_s[idx, :, q0:q1] + jnp.sum(_col_tree(p, jnp.add), axis=0, keepdims=True)
            acc_s[idx, :, q0:q1] = alpha * acc_s[idx, :, q0:q1] + _dot(vt, p.astype(BF16))
            m_s[idx, :, q0:q1] = m_new

    def step(masked):
        half = tq // 2
        pieces = [(0, half, half), (half, tq, tq)] if masked else [(0, tq, tq)]
        scores(0, pieces, masked)
        for n in range(ATTN_MAPS):
            if n + 1 < ATTN_MAPS:
                scores(n + 1, pieces, masked)
            softmax_pv(n, pieces)

    @pl.when(j < i)
    def _():
        step(False)

    @pl.when(j == i)
    def _():
        step(True)
        outs = []
        for h in range(MLA_HEADS):
            lat_t = (acc_s[h] / l_s[h]).astype(BF16)
            outs.append(_dot(wuvt_ref[h], lat_t))
        lam = _lam(lq1_ref, lk1_ref, lq2_ref, lk2_ref, lam_init)
        for gr in range(DIFF_HEADS):
            i0 = MLA_HEADS + 2 * gr
            d = acc_s[i0] / l_s[i0] - lam * (acc_s[i0 + 1] / l_s[i0 + 1])
            d = d * lax.rsqrt(jnp.mean(d * d, axis=0, keepdims=True) + EPS) * gsub_ref[...]
            outs.append(d * (1.0 - lam_init))
        out_ref[...] = jnp.concatenate(outs, axis=0).T.astype(out_ref.dtype)


def _attn(qf, dq0, dq1, kf, ckvt, dkb, dvt, wuvt, gsub_col, lams, lam_init, tq):
    b, s, _ = qf.shape
    nq = s // tq
    assert s % tq == 0 and tq % (2 * LANES) == 0 and (tq // LANES) & (tq // LANES - 1) == 0

    def q_spec(w):
        return pl.BlockSpec((None, tq, w), lambda bb, i, j: (bb, i, 0))

    def k_spec(w):
        return pl.BlockSpec((None, tq, w), lambda bb, i, j: (bb, jnp.minimum(i, j), 0))

    def kt_spec(w):
        return pl.BlockSpec((None, w, tq), lambda bb, i, j: (bb, 0, jnp.minimum(i, j)))

    mix_w = MLA_HEADS * MLA_V + DIFF_HEADS * DIFF_VD
    return pl.pallas_call(
        functools.partial(_attn_kernel, tq=tq, lam_init=lam_init),
        grid=(b, nq, nq),
        in_specs=[q_spec(QK_W), q_spec(DQ_W), q_spec(DQ_W), k_spec(QK_W), kt_spec(MLA_KV_RANK), k_spec(DK_W),
                  kt_spec(DV_W), _const_spec(wuvt.shape), _const_spec(gsub_col.shape)]
                 + [_const_spec(l.shape) for l in lams],
        out_specs=q_spec(mix_w),
        out_shape=jax.ShapeDtypeStruct((b, s, mix_w), BF16),
        scratch_shapes=[pltpu.VMEM((ATTN_MAPS, 1, tq), F32), pltpu.VMEM((ATTN_MAPS, 1, tq), F32),
                        pltpu.VMEM((ATTN_MAPS, DIFF_VD, tq), F32), pltpu.VMEM((ATTN_SLOTS, tq, tq), F32)],
        compiler_params=_cparams(("parallel", "parallel", "arbitrary")),
        name="attn",
    )(qf, dq0, dq1, kf, ckvt, dkb, dvt, wuvt, gsub_col, *lams)


def _qabs_kernel(qf_ref, gk_ref, wt_ref, qa_ref):
    for h in range(MLA_HEADS):
        sl = slice(h * LANES, (h + 1) * LANES)
        qg = (qf_ref[:, sl].astype(F32) * gk_ref[...]).astype(BF16)
        qa_ref[:, sl] = _dot(qg, wt_ref[h])


def _qabs(qf, gk_row, wuk_t):
    n = qf.shape[0]
    return pl.pallas_call(
        _qabs_kernel,
        grid=(1,),
        in_specs=[_const_spec(qf.shape), _const_spec(gk_row.shape), _const_spec(wuk_t.shape)],
        out_specs=_const_spec((n, QK_W)),
        out_shape=jax.ShapeDtypeStruct((n, QK_W), F32),
        compiler_params=_cparams(("arbitrary",)),
        name="qabs",
    )(qf, gk_row, wuk_t)


def _dec_pipeline(refs, *, layer, n_seq, n_pages, lam_init):
    (pt_ref, qa_ref, qp_ref, qbd_ref, qf8_ref, kf8_ref, dkrow_ref, ckvrow_ref, dvrow_ref,
     wukt_ref, wuv_ref, gsub_ref, lq1_ref, lk1_ref, lq2_ref, lk2_ref,
     ckv_hbm, kpe_hbm, kt_hbm, v_hbm, oa_ref, ob_ref,
     ckv_buf, kpe_buf, kt_buf, v_buf, sems, lhs_ref, m_a, l_a, acc_a, m_d, l_d, acc_d) = refs
    pp = DEC_PAGES
    n_steps = n_pages // pp
    total = n_seq * n_steps
    hbm = (ckv_hbm, kpe_hbm, kt_hbm, v_hbm)
    bufs = (ckv_buf, kpe_buf, kt_buf, v_buf)
    nk = MLA_HEADS * MLA_NOPE
    n_stages = DEC_CHAINS + DEC_SKEW

    def page_copy(a, slot, k, page):
        return pltpu.make_async_copy(hbm[a].at[layer, page], bufs[a].at[slot, k], sems.at[a, slot])

    def start_step(t, slot):
        b = t // n_steps
        first = (t - b * n_steps) * pp
        for k in range(pp):
            page = pt_ref[b, first + k]
            for a in range(len(hbm)):
                page_copy(a, slot, k, page).start()

    def wait_step(slot):
        for k in range(pp):
            for a in range(len(hbm)):
                page_copy(a, slot, k, 0).wait()

    def update(s, m_ref, l_ref, c):
        m_prev = m_ref[c]
        m_new = jnp.maximum(m_prev, jnp.max(s, axis=-1, keepdims=True))
        alpha = jnp.exp2(m_prev - m_new)
        p = jnp.exp2(s - m_new)
        l_ref[c] = alpha * l_ref[c] + jnp.sum(p, axis=-1, keepdims=True)
        m_ref[c] = m_new
        return alpha, p

    def pages(c):
        return range(c * DEC_CHAIN_PAGES, (c + 1) * DEC_CHAIN_PAGES)

    def stages(t):
        slot = t & 1
        b = t // n_steps
        j = t - b * n_steps
        row = lax.broadcasted_iota(jnp.int32, (MLA_HEADS, 1), 0)
        row_g0 = ((row >> 1) & 1) == 0
        prods = {}

        def begin():
            start_step(jnp.minimum(t + 1, total - 1), 1 - slot)
            wait_step(slot)

            @pl.when(j == 0)
            def _():
                lhs_ref[nk:, :] = jnp.concatenate(
                    [qa_ref[b].astype(BF16), jnp.zeros((lhs_ref.shape[0] - nk - MLA_HEADS, LANES), BF16)], axis=0)
                for r in (m_a, m_d):
                    r[...] = jnp.full(r.shape, NEG, F32)
                for r in (l_a, acc_a, l_d, acc_d):
                    r[...] = jnp.zeros(r.shape, F32)

        def score_products(c):
            ckv = jnp.concatenate([ckv_buf[slot, k] for k in pages(c)], axis=0).astype(BF16)
            res = _dot_nt(lhs_ref[...], ckv)
            kpe_t = jnp.concatenate([kpe_buf[slot, k] for k in pages(c)], axis=1).astype(BF16)
            bp = _dot(qp_ref[b], kpe_t)
            kt = jnp.concatenate([kt_buf[slot, k] for k in pages(c)], axis=1).astype(BF16)
            return ckv, res, bp, _dot(qbd_ref[b], kt)

        def softmax(c, res, bp, sd):
            sq = res[:nk] * res[:nk]
            ssq = jnp.concatenate(
                [jnp.sum(sq[h * MLA_NOPE:(h + 1) * MLA_NOPE], axis=0, keepdims=True) for h in range(MLA_HEADS)],
                axis=0)
            rnorm = lax.rsqrt(ssq * (1.0 / MLA_NOPE) + EPS)
            alpha_a, p_a = update(res[nk:nk + MLA_HEADS] * rnorm + bp, m_a, l_a, c)
            alpha_d, p_d = update(sd, m_d, l_d, c)
            return alpha_a, p_a.astype(BF16), alpha_d, p_d.astype(BF16)

        def value_products(c, ckv, alpha_a, p_a, alpha_d, p_d):
            acc_a[c] = alpha_a * acc_a[c] + _dot(p_a, ckv)
            pv = []
            for g in range(DIFF_KV_HEADS):
                v = jnp.concatenate(
                    [v_buf[slot, k, pl.ds(g, PAGE_SIZE, stride=DIFF_KV_HEADS), :] for k in pages(c)], axis=0)
                pv.append(_dot(p_d, v.astype(BF16)))
            acc_d[c] = alpha_d * acc_d[c] + jnp.where(row_g0, pv[0], pv[1])

        def merged(s_self, v_self, m_ref, l_ref, acc_ref):
            m = s_self
            for c in range(DEC_CHAINS):
                m = jnp.maximum(m, m_ref[c])
            p_self = jnp.exp2(s_self - m)
            l = p_self
            acc = p_self * v_self
            for c in range(DEC_CHAINS):
                w = jnp.exp2(m_ref[c] - m)
                l = l + w * l_ref[c]
                acc = acc + w * acc_ref[c]
            return acc / l

        def finish():
            @pl.when(j == n_steps - 1)
            def _():
                s_self = jnp.sum(qf8_ref[b].astype(F32) * kf8_ref[b].astype(F32), axis=-1, keepdims=True)
                lat = merged(s_self, ckvrow_ref[b], m_a, l_a, acc_a)
                full = _dot(lat.astype(BF16), wuv_ref[...])
                col_head = lax.broadcasted_iota(jnp.int32, full.shape, 1) >> 6
                row_head = lax.broadcasted_iota(jnp.int32, full.shape, 0)
                oa_ref[b] = jnp.sum(jnp.where(col_head == row_head, full, 0.0), axis=0, keepdims=True)
                s_self = jnp.sum(qbd_ref[b].astype(F32) * dkrow_ref[b].astype(F32), axis=-1, keepdims=True)
                dvrow = dvrow_ref[b]
                v_self = jnp.where(row_g0, dvrow[:, :DIFF_VD], dvrow[:, DIFF_VD:])
                o = merged(s_self, v_self, m_d, l_d, acc_d)
                lam = _lam(lq1_ref, lk1_ref, lq2_ref, lk2_ref, lam_init)
                d = o[:DIFF_HEADS] - lam * o[DIFF_HEADS:]
                ob_ref[b] = _rms(d, gsub_ref[...]) * (1.0 - lam_init)

        def make(step):
            def run():
                if step == 0:
                    begin()
                if step < DEC_CHAINS:
                    prods[step] = score_products(step)
                c = step - DEC_SKEW
                if c >= 0:
                    ckv, res, bp, sd = prods.pop(c)
                    value_products(c, ckv, *softmax(c, res, bp, sd))
                if step == n_stages - 1:
                    finish()
            return run

        return [make(step) for step in range(n_stages)]

    def prime():
        lhs_ref[:nk, :] = wukt_ref[...]
        start_step(0, 0)

    def drain():
        wait_step(total & 1)

    return prime, drain, stages, total


def _dec_scratch(caches):
    return ([pltpu.VMEM((2, DEC_PAGES) + c.shape[2:], c.dtype) for c in caches]
            + [pltpu.SemaphoreType.DMA((len(caches), 2)),
               pltpu.VMEM((MLA_HEADS * MLA_NOPE + 16, LANES), BF16),
               pltpu.VMEM((DEC_CHAINS, MLA_HEADS, 1), F32), pltpu.VMEM((DEC_CHAINS, MLA_HEADS, 1), F32),
               pltpu.VMEM((DEC_CHAINS, MLA_HEADS, MLA_KV_RANK), F32),
               pltpu.VMEM((DEC_CHAINS, MLA_HEADS, 1), F32), pltpu.VMEM((DEC_CHAINS, MLA_HEADS, 1), F32),
               pltpu.VMEM((DEC_CHAINS, MLA_HEADS, DIFF_VD), F32)])


def _dec_kernel(*refs, layer, n_seq, n_pages, lam_init):
    prime, drain, stages, total = _dec_pipeline(
        refs, layer=layer, n_seq=n_seq, n_pages=n_pages, lam_init=lam_init)

    def body(t, carry):
        for run in stages(t):
            run()
        return carry

    prime()
    lax.fori_loop(0, total, body, 0)
    drain()


def _decode(page_table, per_seq, consts, caches, layer, lam_init):
    b, n_pages = page_table.shape

    def full_spec(a):
        return pl.BlockSpec(a.shape, lambda i, pt: (0,) * a.ndim)

    oa_w = MLA_HEADS * MLA_V
    out_shape = [jax.ShapeDtypeStruct((b, 1, oa_w), F32), jax.ShapeDtypeStruct((b, DIFF_HEADS, DIFF_VD), F32)]
    grid_spec = pltpu.PrefetchScalarGridSpec(
        num_scalar_prefetch=1,
        grid=(1,),
        in_specs=[full_spec(a) for a in list(per_seq) + list(consts)]
                 + [pl.BlockSpec(memory_space=pl.ANY)] * len(caches),
        out_specs=[full_spec(o) for o in out_shape],
        scratch_shapes=_dec_scratch(caches),
    )
    return pl.pallas_call(
        functools.partial(_dec_kernel, layer=layer, n_seq=b, n_pages=n_pages, lam_init=lam_init),
        grid_spec=grid_spec,
        out_shape=out_shape,
        compiler_params=_cparams(("arbitrary",)),
        name="decode",
    )(page_table, *per_seq, *consts, *caches)


def _out_kernel(mix_ref, x_ref, gt_ref, sc_ref, sh_ref, wo_ref, g2_ref, wrh_ref, wrl_ref, br_ref,
                x1_ref, h2_ref, gate_ref):
    o = _dot(mix_ref[...], wo_ref[...])
    x1 = x_ref[...] + gt_ref[...] * o
    x1_ref[...] = x1
    h2 = _rms(x1, g2_ref[...]) * (1.0 + sc_ref[...]) + sh_ref[...]
    h2_ref[...] = h2.astype(BF16)
    hh, hl = _split(h2)
    logits = _dot(hh, wrh_ref[...]) + _dot(hh, wrl_ref[...]) + _dot(hl, wrh_ref[...]) + br_ref[...]
    lane_i = lax.broadcasted_iota(jnp.int32, logits.shape, 1)
    lane = lane_i.astype(F32)
    big = float(ROUTER_LANES)
    gl = jnp.where(lane_i < N_GROUPS, logits, NEG)
    gmax = jnp.max(gl, axis=-1, keepdims=True)
    gidx = jnp.min(jnp.where(gl == gmax, lane, big), axis=-1, keepdims=True)
    g_w = 1.0 / jnp.sum(jnp.exp(gl - gmax), axis=-1, keepdims=True)
    in_group = (lane_i >= N_GROUPS) & (lane_i < N_GROUPS + N_ROUTED) & (
        ((lane_i - N_GROUPS) >> 3).astype(F32) == gidx)
    el = jnp.where(in_group, logits, NEG)
    e1 = jnp.max(el, axis=-1, keepdims=True)
    i1 = jnp.min(jnp.where(el == e1, lane, big), axis=-1, keepdims=True)
    el2 = jnp.where(lane == i1, NEG, el)
    e2 = jnp.max(el2, axis=-1, keepdims=True)
    i2 = jnp.min(jnp.where(el2 == e2, lane, big), axis=-1, keepdims=True)
    t = jnp.exp(e2 - e1)
    w1 = 1.0 / (1.0 + t)
    w2 = t / (1.0 + t)
    gate_ref[...] = jnp.where(lane == i1, w1, jnp.where(lane == i2, w2, 0.0)) * g_w


def _out(mix, x3, gt, sc, sh, wo, g2, wrh, wrl, br, tm):
    b, t, d = x3.shape
    per_tok = gt.shape[1] != 1

    def tok_spec(w):
        return pl.BlockSpec((None, tm, w), lambda s, bb: (bb, s, 0))

    mod_spec = tok_spec(d) if per_tok else pl.BlockSpec((None, 1, d), lambda s, bb: (bb, 0, 0))
    consts = [wo, g2, wrh, wrl, br]
    return pl.pallas_call(
        _out_kernel,
        grid=(t // tm, b),
        in_specs=[tok_spec(mix.shape[2]), tok_spec(d), mod_spec, mod_spec, mod_spec] + [_const_spec(c.shape) for c in consts],
        out_specs=[tok_spec(d), tok_spec(d), tok_spec(ROUTER_LANES)],
        out_shape=[jax.ShapeDtypeStruct((b, t, d), F32), jax.ShapeDtypeStruct((b, t, d), BF16),
                   jax.ShapeDtypeStruct((b, t, ROUTER_LANES), F32)],
        compiler_params=_cparams(("parallel", "parallel")),
        name="out",
    )(mix, x3, gt, sc, sh, *consts)


MOE_CHUNK = MXU_DIM


def _moe_kernel(h2_ref, gate_ref, x1_ref, gt_ref, wg_ref, wu_ref, wd_ref, ex_ref, y_ref):
    g = pl.program_id(2)

    @pl.when(g == 0)
    def _():
        y_ref[...] = jnp.zeros(y_ref.shape, F32)

    h = h2_ref[...]
    gate_b = gate_ref[...].astype(BF16)
    for c in range(GROUP_FF // MOE_CHUNK):
        cs = slice(c * MOE_CHUNK, (c + 1) * MOE_CHUNK)
        ge = _dot(gate_b, ex_ref[:, cs])
        a = _silu(_dot(h, wg_ref[:, cs])) * _dot(h, wu_ref[:, cs]) * ge
        y_ref[...] += _dot(a.astype(BF16), wd_ref[cs, :])

    @pl.when(g == N_GROUPS - 1)
    def _():
        y_ref[...] = x1_ref[...] + gt_ref[...] * y_ref[...]


def _moe(h2, gate, x1, gt, wg, wu, wd, ex, tm):
    b, t, d = x1.shape
    per_tok = gt.shape[1] != 1

    def tok_spec(w):
        return pl.BlockSpec((None, tm, w), lambda s, bb, g: (bb, s, 0))

    mod_spec = tok_spec(d) if per_tok else pl.BlockSpec((None, 1, d), lambda s, bb, g: (bb, 0, 0))

    def grp_spec(a):
        return pl.BlockSpec((None,) + a.shape[1:], lambda s, bb, g: (g, 0, 0))

    return pl.pallas_call(
        _moe_kernel,
        grid=(t // tm, b, N_GROUPS),
        in_specs=[tok_spec(d), tok_spec(ROUTER_LANES), tok_spec(d), mod_spec,
                  grp_spec(wg), grp_spec(wu), grp_spec(wd), grp_spec(ex)],
        out_specs=tok_spec(d),
        out_shape=jax.ShapeDtypeStruct((b, t, d), F32),
        compiler_params=_cparams(("parallel", "parallel", "arbitrary")),
        name="moe",
    )(h2, gate, x1, gt, wg, wu, wd, ex)


def _token_tiles(seq):
    return min(512, seq), min(512, seq), min(512, seq), min(1024, seq)


def _rope_tables(pos):
    def cs(dim):
        half = dim // 2
        inv = ROPE_THETA ** (-jnp.arange(half, dtype=F32) * 2.0 / dim)
        ang = pos[:, None] * inv[None, :]
        c, s = jnp.cos(ang), jnp.sin(ang)
        return jnp.concatenate([c, c], axis=1), jnp.concatenate([-s, s], axis=1)

    t = pos.shape[0]
    c32, s32 = cs(MLA_ROPE)
    c64, s64 = cs(DIFF_HD)
    z = lambda w: jnp.zeros((t, w), F32)
    pad = LANES - MLA_NOPE - MLA_ROPE
    cosq = jnp.concatenate([jnp.ones((t, MLA_NOPE), F32), c32, z(pad)], axis=1)
    sinq = jnp.concatenate([z(MLA_NOPE), s32, z(pad)], axis=1)
    cosk = jnp.concatenate([c32, z(LANES - MLA_ROPE)], axis=1)
    sink = jnp.concatenate([s32, z(LANES - MLA_ROPE)], axis=1)
    cosd = jnp.concatenate([c64, c64], axis=1)
    sind = jnp.concatenate([s64, s64], axis=1)
    return [cosq, sinq, cosk, sink, cosd, sind]


def _block_diag_mean(sizes, width):
    m = jnp.zeros((width, width), F32)
    o = 0
    while o < width:
        for sz in sizes:
            if sz > 0:
                m = m.at[o:o + sz, o:o + sz].set(1.0 / sz)
            o += abs(sz)
    return m.astype(BF16)


def _layer_weights(l, w_in, g_norm1, g_mla_qa, w_mla_uq, g_mla_kva, w_mla_uk, g_mla_qn_nope, g_mla_qn_rope,
                   g_mla_kn_nope, g_mla_kn_rope, g_diff_qn, g_diff_kn):
    d = w_in.shape[1]
    o_kpe = MLA_Q_RANK + MLA_KV_RANK
    wi = w_in[l]
    win = jnp.concatenate([wi[:, :o_kpe], wi[:, o_kpe:o_kpe + MLA_ROPE], jnp.zeros((d, LANES - MLA_ROPE), F32),
                           wi[:, o_kpe + MLA_ROPE:]], axis=1).astype(BF16)
    pad = LANES - MLA_NOPE - MLA_ROPE
    wuq = w_mla_uq[l].reshape(MLA_Q_RANK, MLA_HEADS, MLA_NOPE + MLA_ROPE)
    wuq = jnp.concatenate([wuq, jnp.zeros((MLA_Q_RANK, MLA_HEADS, pad), F32)], axis=2).reshape(MLA_Q_RANK, QK_W).astype(BF16)
    wuk = jnp.concatenate([w_mla_uk[l], jnp.zeros((MLA_KV_RANK, MLA_HEADS, LANES - MLA_NOPE), F32)], axis=2)
    wuk = wuk.reshape(MLA_KV_RANK, QK_W).astype(BF16)
    gq = jnp.tile(jnp.concatenate([g_mla_qn_nope[l], g_mla_qn_rope[l], jnp.zeros((pad,), F32)]), MLA_HEADS)[None]
    gk = jnp.tile(jnp.concatenate([g_mla_kn_nope[l], jnp.zeros((LANES - MLA_NOPE,), F32)]), MLA_HEADS)[None]
    gkpe = jnp.concatenate([g_mla_kn_rope[l], jnp.zeros((LANES - MLA_ROPE,), F32)])[None]
    gdq = jnp.tile(g_diff_qn[l], DQ_W // DIFF_HD)[None]
    gdk = jnp.tile(g_diff_kn[l], DK_W // DIFF_HD)[None]
    bdq = _block_diag_mean((MLA_NOPE, MLA_ROPE, -pad), MXU_DIM)
    bdd = _block_diag_mean((DIFF_HD,), MXU_DIM)
    return [g_norm1[l][None], win, g_mla_qa[l][None], wuq, gq, g_mla_kva[l][None], wuk, gk, gkpe, gdq, gdk, bdq, bdd]


def kernel(x_prompt, x_sample, cache_mla_ckv, cache_mla_kpe, cache_diff_k, cache_diff_v, page_table, c_prompt, c_sample, w_ada, b_ada, g_norm1, w_in, g_mla_qa, w_mla_uq, g_mla_kva, w_mla_uk, w_mla_uv, g_mla_qn_nope, g_mla_qn_rope, g_mla_kn_nope, g_mla_kn_rope, g_diff_qn, g_diff_kn, lam_q1, lam_k1, lam_q2, lam_k2, g_diff_subln, w_o, g_norm2, w_router_group, b_router_group, w_router_expert, b_router_expert, w_exp_gate, w_exp_up, w_exp_down):
    bp, sp, d = x_prompt.shape
    bs, ts, _ = x_sample.shape
    depth = w_in.shape[0]
    n_pool = cache_mla_ckv.shape[1]
    n_pages = page_table.shape[1]
    assert ts == 1 and n_pages % DEC_PAGES == 0 and cache_mla_ckv.shape[2] == PAGE_SIZE
    assert w_in.shape[2] == PROJ_W - LANES + MLA_ROPE and d % MXU_DIM == 0
    past = n_pages * PAGE_SIZE

    tm_p, tq, tm_o, tm_m = _token_tiles(sp)
    tables_p = _rope_tables(jnp.arange(sp, dtype=F32))
    tables_s = _rope_tables(jnp.arange(ts, dtype=F32) + past)

    kpe_t = jnp.transpose(cache_mla_kpe, (0, 1, 3, 2))
    k_t = jnp.transpose(cache_diff_k, (0, 1, 3, 4, 5, 2)).reshape(depth, n_pool, DK_W, PAGE_SIZE)
    v_rows = cache_diff_v.reshape(depth, n_pool, PAGE_SIZE * DIFF_KV_HEADS, DIFF_VD)

    xp = x_prompt
    xs = x_sample.reshape(1, bs, d)
    outs_p = [[], [], [], []]
    outs_s = [[], [], [], []]
    for l in range(depth):
        lam_init = 0.8 - 0.6 * math.exp(-0.3 * l)
        wts = _layer_weights(l, w_in, g_norm1, g_mla_qa, w_mla_uq, g_mla_kva, w_mla_uk, g_mla_qn_nope,
                             g_mla_qn_rope, g_mla_kn_nope, g_mla_kn_rope, g_diff_qn, g_diff_kn)
        lams = [lam_q1[l][None], lam_k1[l][None], lam_q2[l][None], lam_k2[l][None]]
        gsub = g_diff_subln[l][None]
        wuv = w_mla_uv[l]
        wuv_flat = wuv.reshape(MLA_KV_RANK, MLA_HEADS * MLA_V).astype(BF16)
        wuvt = jnp.transpose(wuv, (1, 2, 0)).astype(BF16)
        wuk_t = jnp.concatenate([jnp.transpose(w_mla_uk[l], (1, 2, 0)),
                                 jnp.zeros((MLA_HEADS, LANES - MLA_NOPE, MLA_KV_RANK), F32)], axis=1).astype(BF16)
        wukt_rows = jnp.transpose(w_mla_uk[l], (1, 2, 0)).reshape(MLA_HEADS * MLA_NOPE, MLA_KV_RANK).astype(BF16)
        gk_row = jnp.concatenate([g_mla_kn_nope[l], jnp.zeros((LANES - MLA_NOPE,), F32)])[None]
        wo = w_o[l].astype(BF16)
        wr = jnp.concatenate([w_router_group[l], jnp.transpose(w_router_expert[l], (1, 0, 2)).reshape(d, N_ROUTED),
                              jnp.zeros((d, ROUTER_LANES - N_GROUPS - N_ROUTED), F32)], axis=1)
        wrh, wrl = _split(wr)
        br = jnp.concatenate([b_router_group[l], b_router_expert[l].reshape(N_ROUTED),
                              jnp.zeros((ROUTER_LANES - N_GROUPS - N_ROUTED,), F32)])[None]
        wg = jnp.transpose(w_exp_gate[l], (0, 2, 1, 3)).reshape(N_GROUPS, d, GROUP_FF).astype(BF16)
        wu = jnp.transpose(w_exp_up[l], (0, 2, 1, 3)).reshape(N_GROUPS, d, GROUP_FF).astype(BF16)
        wd = w_exp_down[l].reshape(N_GROUPS, GROUP_FF, d).astype(BF16)
        lane = jnp.arange(ROUTER_LANES)[None, :, None]
        col = jnp.arange(GROUP_FF)[None, None, :]
        grp = jnp.arange(N_GROUPS)[:, None, None]
        ex = (lane == N_GROUPS + grp * EXPERTS_PER_GROUP + col // EXPERT_FF).astype(BF16)

        mod = _ada(jnp.concatenate([c_prompt, c_sample], axis=0), w_ada[l], b_ada[l][None])
        mod_p = mod[:bp].reshape(bp, 6, 1, d)
        mod_s = mod[bp:].reshape(1, bs, 6, d)
        sh1p, sc1p, gt1p, sh2p, sc2p, gt2p = [mod_p[:, k] for k in range(6)]
        sh1s, sc1s, gt1s, sh2s, sc2s, gt2s = [mod_s[:, :, k] for k in range(6)]

        (qf, kf, dq0, dq1, ckvt, dkb, dvt, ckv, kpe, dk, dv) = _proj(xp, sc1p, sh1p, tables_p, wts, tm_p)
        mix = _attn(qf, dq0, dq1, kf, ckvt, dkb, dvt, wuvt, gsub.reshape(DIFF_VD, 1), lams, lam_init, tq)
        x1, h2, gate = _out(mix, xp, gt1p, sc2p, sh2p, wo, g_norm2[l][None], wrh, wrl, br, tm_o)
        for lst, a in zip(outs_p, (ckv, kpe, dk, dv)):
            lst.append(a)
        (qf, kf, dq0, dq1, _, dkb, _, ckv, kpe, dk, dv) = _proj(xs, sc1s, sh1s, tables_s, wts, bs)
        for lst, a in zip(outs_s, (ckv, kpe, dk, dv)):
            lst.append(a)
        qf2 = qf.reshape(bs, QK_W)
        qa = _qabs(qf2, gk_row, wuk_t).reshape(bs, MLA_HEADS, LANES)
        qf8 = qf2.reshape(bs, MLA_HEADS, LANES)
        kf8 = kf.reshape(bs, MLA_HEADS, LANES)
        qp = qf8[:, :, MLA_NOPE:MLA_NOPE + MLA_ROPE]
        dq5 = (dq0 + dq1).reshape(bs, DIFF_KV_HEADS, 2, 2, DIFF_HD)
        eye = jnp.eye(2, dtype=BF16)
        qbd = jnp.einsum('bgrmd,gh,mn->bmgrhnd', dq5, eye, eye).reshape(bs, 2 * DIFF_HEADS, DK_W)
        per_seq = [qa, qp, qbd, qf8, kf8, dkb.reshape(bs, 1, DK_W), ckv.reshape(bs, 1, MLA_KV_RANK),
                   dv.reshape(bs, 1, DV_W)]
        consts = [wukt_rows, wuv_flat, gsub] + lams
        caches = (cache_mla_ckv, kpe_t, k_t, v_rows)

        xp = _moe(h2, gate, x1, gt2p, wg, wu, wd, ex, tm_m)
        oa, ob = _decode(page_table, per_seq, consts, caches, l, lam_init)
        mix = jnp.concatenate([oa.reshape(bs, -1), ob.reshape(bs, -1)], axis=1).astype(BF16).reshape(1, bs, -1)
        x1, h2, gate = _out(mix, xs, gt1s, sc2s, sh2s, wo, g_norm2[l][None], wrh, wrl, br, bs)
        xs = _moe(h2, gate, x1, gt2s, wg, wu, wd, ex, bs)

    def stack_p(lst, tail):
        return jnp.stack(lst).reshape((depth, bp, sp) + tail)

    def stack_s(lst, tail):
        return jnp.stack(lst).reshape((depth, bs, ts) + tail)

    k_tail = (DIFF_KV_HEADS, 2, DIFF_HD)
    v_tail = (DIFF_KV_HEADS, DIFF_VD)
    return (xp, xs.reshape(bs, ts, d),
            stack_p(outs_p[0], (MLA_KV_RANK,)), stack_p(outs_p[1], (MLA_ROPE,)), stack_p(outs_p[2], k_tail),
            stack_p(outs_p[3], v_tail),
            stack_s(outs_s[0], (MLA_KV_RANK,)), stack_s(outs_s[1], (MLA_ROPE,)), stack_s(outs_s[2], k_tail),
            stack_s(outs_s[3], v_tail))
```
